```python
import math
import jax, jax.numpy as jnp
from jax import lax
import numpy as np


D_MODEL = 2048
BATCH = 4
SEQ = 2048
DEPTH = 1
DEC_BATCH = 128
DEC_SEQ = 1
PAST_LEN = 2048
PAGE_SIZE = 128

EPS = 1e-6
CHUNK = 128
A_GROUPS = 8
A_GROUP_CH = 128
A_WIDTH = A_GROUPS * A_GROUP_CH
HEAD_DIM = 128
HEADS_PER_GROUP = 4
DIL_GROUPS = ((128, 1), (512, 4), (2048, 16))
N_DIL = 3
B_HEADS = N_DIL * HEADS_PER_GROUP
B_WIDTH = B_HEADS * HEAD_DIM
B_OUT_WIDTH = HEADS_PER_GROUP * HEAD_DIM
Q_BLOCK = 128
ATTN_SCALE = HEAD_DIM ** -0.5
ROPE_THETA = 500000.0
ROT_DIM = HEAD_DIM // 4
IN_COLS = 2 * A_WIDTH + 3 * B_WIDTH + 2 * D_MODEL
IN_SPLITS = (A_WIDTH, 2 * A_WIDTH, 2 * A_WIDTH + B_WIDTH, 2 * A_WIDTH + 2 * B_WIDTH,
             2 * A_WIDTH + 3 * B_WIDTH, 2 * A_WIDTH + 3 * B_WIDTH + D_MODEL)
N_KEYS = 128
N_EXPERTS = N_KEYS * N_KEYS
PEER_HEADS = 8
PEER_KEY_DIM = 256
PEER_HALF = PEER_KEY_DIM // 2
PEER_TOPK = 16
PEER_TOKEN_BLOCK = 128
PLE_DIM = 256

kernel_name = "hybrid_sgu_dilattn_peer_decoder_step"


def rmsnorm(x, g):
    xf = x.astype(jnp.float32)
    r = lax.rsqrt(jnp.mean(xf * xf, axis=-1, keepdims=True) + EPS)
    return (xf * r).astype(x.dtype) * g


def layernorm(x, g, b):
    xf = x.astype(jnp.float32)
    mu = jnp.mean(xf, axis=-1, keepdims=True)
    var = jnp.mean(jnp.square(xf - mu), axis=-1, keepdims=True)
    return ((xf - mu) * lax.rsqrt(var + EPS)).astype(x.dtype) * g + b


def partial_rope(x, pos):
    half = ROT_DIM // 2
    inv = ROPE_THETA ** (-jnp.arange(half, dtype=jnp.float32) / half)
    ang = pos.astype(jnp.float32)[:, None] * inv[None, :]
    cos = jnp.cos(ang)[:, None, :].astype(x.dtype)
    sin = jnp.sin(ang)[:, None, :].astype(x.dtype)
    x1, x2, rest = x[..., :half], x[..., half:ROT_DIM], x[..., ROT_DIM:]
    return jnp.concatenate([x1 * cos - x2 * sin, x2 * cos + x1 * sin, rest], axis=-1)


def softmax_lse(s):
    m = jnp.max(s, axis=-1, keepdims=True)
    e = jnp.exp(s - m)
    den = jnp.sum(e, axis=-1, keepdims=True)
    return e / den, (m + jnp.log(den))[..., 0]


def chunk_spatial_gating(u, v, w_s, b_s):
    bn, t, _ = v.shape
    n_chunk = -(-t // CHUNK)
    vp = jnp.pad(v, ((0, 0), (0, n_chunk * CHUNK - t), (0, 0)))
    vp = vp.reshape(bn, n_chunk, CHUNK, A_GROUPS, A_GROUP_CH)
    causal = jnp.tril(jnp.ones((CHUNK, CHUNK), dtype=bool))
    ws = jnp.where(causal[None], w_s, 0.0)
    mixed = jnp.einsum('gts,bcsgd->bctgd', ws, vp) + b_s.T[None, None, :, :, None]
    mixed = mixed.reshape(bn, n_chunk * CHUNK, A_WIDTH)[:, :t]
    return u * mixed


def dilated_group_prompt(q, k, v, window, dilation):
    bn, s_len, h, dh = q.shape
    nw = window // dilation
    ls = s_len // dilation
    qb = math.gcd(ls, Q_BLOCK)
    nblk = ls // qb
    strided = lambda t: t.reshape(bn, ls, dilation, h, dh).transpose(0, 2, 1, 3, 4)
    qs = strided(q).reshape(bn, dilation, nblk, qb, h, dh)
    pad = ((0, 0), (0, 0), (nw, 0), (0, 0), (0, 0))
    kp = jnp.pad(strided(k), pad)
    vp = jnp.pad(strided(v), pad)
    blk = jnp.arange(nblk)[:, None]
    col = jnp.arange(qb + nw)[None, :]
    idx = blk * qb + col
    kb = jnp.take(kp, idx, axis=2)
    vb = jnp.take(vp, idx, axis=2)
    s = jnp.einsum('brnqhd,brnkhd->brnhqk', qs, kb).astype(jnp.float32) * ATTN_SCALE
    dist = jnp.arange(qb)[:, None] - col + nw
    keypos = blk[:, :, None] * qb + col[None] - nw
    valid = (dist >= 0) & (dist <= nw) & (keypos >= 0)
    s = jnp.where(valid[None, None, :, None], s, -jnp.inf)
    prob, lse = softmax_lse(s)
    o = jnp.einsum('brnhqk,brnkhd->brnqhd', prob.astype(v.dtype), vb)
    o = o.reshape(bn, dilation, ls, h, dh).transpose(0, 2, 1, 3, 4).reshape(bn, s_len, h, dh)
    lse = lse.transpose(0, 1, 2, 4, 3).reshape(bn, dilation, ls, h)
    lse = lse.transpose(0, 2, 1, 3).reshape(bn, s_len, h)
    return o, lse


def dilated_group_sample(q, k_new, v_new, k_cache, v_cache, window, dilation):
    t = q.shape[1]
    l_buf = k_cache.shape[1]
    nw = window // dilation
    k_all = jnp.concatenate([k_cache, k_new], axis=1)
    v_all = jnp.concatenate([v_cache, v_new], axis=1)
    idx = l_buf + jnp.arange(t)[:, None] - dilation * jnp.arange(nw + 1)[None, :]
    valid = idx >= 0
    idx_c = jnp.maximum(idx, 0)
    kg = jnp.take(k_all, idx_c, axis=1)
    vg = jnp.take(v_all, idx_c, axis=1)
    s = jnp.einsum('nthd,ntkhd->nthk', q, kg).astype(jnp.float32) * ATTN_SCALE
    s = jnp.where(valid[None, :, None, :], s, -jnp.inf)
    prob, lse = softmax_lse(s)
    o = jnp.einsum('nthk,ntkhd->nthd', prob.astype(v_new.dtype), vg)
    return o, lse


def combine_dilations(outs, lses):
    o = jnp.stack(outs, axis=0)
    w = jax.nn.softmax(jnp.stack(lses, axis=0), axis=0)
    return jnp.sum(w[..., None].astype(o.dtype) * o, axis=0)


def peer_ffn(h, w_q, sub_k1, sub_k2, u_tab, v_tab):
    shp = h.shape
    xt = h.reshape(-1, D_MODEL)
    n_tok = xt.shape[0]
    nb = -(-n_tok // PEER_TOKEN_BLOCK)
    xt = jnp.pad(xt, ((0, nb * PEER_TOKEN_BLOCK - n_tok), (0, 0)))
    xt = xt.reshape(nb, PEER_TOKEN_BLOCK, D_MODEL)

    def block(xb):
        q = (xb @ w_q).reshape(PEER_TOKEN_BLOCK, PEER_HEADS, 2, PEER_HALF)
        s1 = jnp.einsum('thd,kd->thk', q[:, :, 0], sub_k1).astype(jnp.float32)
        s2 = jnp.einsum('thd,kd->thk', q[:, :, 1], sub_k2).astype(jnp.float32)
        t1, i1 = lax.top_k(s1, PEER_TOPK)
        t2, i2 = lax.top_k(s2, PEER_TOPK)
        cand = (t1[..., :, None] + t2[..., None, :]).reshape(PEER_TOKEN_BLOCK, PEER_HEADS, PEER_TOPK * PEER_TOPK)
        ts, ic = lax.top_k(cand, PEER_TOPK)
        e1 = jnp.take_along_axis(i1, ic // PEER_TOPK, axis=-1)
        e2 = jnp.take_along_axis(i2, ic % PEER_TOPK, axis=-1)
        expert = e1 * N_KEYS + e2
        g = jax.nn.softmax(ts, axis=-1)
        act = jax.nn.gelu(jnp.einsum('thkd,td->thk', u_tab[expert], xb).astype(jnp.float32))
        return jnp.einsum('thk,thkd->td', (g * act).astype(xb.dtype), v_tab[expert])

    y = lax.map(block, xt).reshape(nb * PEER_TOKEN_BLOCK, D_MODEL)[:n_tok]
    return y.reshape(shp)


def layer(x, p, pos, group_attention, g_mix, w_in, sgu_ln_g, sgu_ln_b, w_s, b_s,
          w_a_out, w_b_out, w_o, g_ffn, peer_w_q, peer_sub_k1, peer_sub_k2,
          peer_u, peer_v, w_ple, w_ple_gate):
    bn, t, _ = x.shape
    h = rmsnorm(x, g_mix)
    u_a, v_a, q, k, v, gate_a, gate_b = jnp.split(h @ w_in, IN_SPLITS, axis=-1)
    u = jax.nn.gelu(u_a)
    vn = layernorm(jax.nn.gelu(v_a), sgu_ln_g, sgu_ln_b)
    a_mix = chunk_spatial_gating(u, vn, w_s, b_s)
    q = partial_rope(q.reshape(bn, t, B_HEADS, HEAD_DIM), pos)
    k = partial_rope(k.reshape(bn, t, B_HEADS, HEAD_DIM), pos)
    v = v.reshape(bn, t, B_HEADS, HEAD_DIM)
    outs, lses = [], []
    for gi in range(N_DIL):
        sl = slice(gi * HEADS_PER_GROUP, (gi + 1) * HEADS_PER_GROUP)
        o, l = group_attention(gi, q[:, :, sl], k[:, :, sl], v[:, :, sl])
        outs.append(o)
        lses.append(l)
    b_mix = combine_dilations(outs, lses).reshape(bn, t, B_OUT_WIDTH)
    merged = jax.nn.sigmoid(gate_a) * (a_mix @ w_a_out) + jax.nn.sigmoid(gate_b) * (b_mix @ w_b_out)
    x = x + merged @ w_o
    x = x + peer_ffn(rmsnorm(x, g_ffn), peer_w_q, peer_sub_k1, peer_sub_k2, peer_u, peer_v)
    x = x + jax.nn.sigmoid(x @ w_ple_gate) * (p @ w_ple)
    return x, k, v, vn


def setup_inputs(seed: int = 0) -> dict:
    key = jax.random.key(seed)
    ks = jax.random.split(key, 32)
    nrm = lambda kk, shape, scale: jax.random.normal(kk, shape, jnp.float32) * scale
    l_buf = [min(w, PAST_LEN) for w, _ in DIL_GROUPS]
    cshape = lambda l: (DEPTH, DEC_BATCH, l, 2, HEADS_PER_GROUP, HEAD_DIM)
    return {
        "x_prompt": nrm(ks[0], (BATCH, SEQ, D_MODEL), 1.0),
        "x_sample": nrm(ks[1], (DEC_BATCH, DEC_SEQ, D_MODEL), 1.0),
        "cache_kv_w128": nrm(ks[2], cshape(l_buf[0]), 1.0),
        "cache_kv_w512": nrm(ks[3], cshape(l_buf[1]), 1.0),
        "cache_kv_w2048": nrm(ks[4], cshape(l_buf[2]), 1.0),
        "p_prompt": nrm(ks[5], (DEPTH, BATCH, SEQ, PLE_DIM), 1.0),
        "p_sample": nrm(ks[6], (DEPTH, DEC_BATCH, DEC_SEQ, PLE_DIM), 1.0),
        "g_mix": 1.0 + nrm(ks[7], (DEPTH, D_MODEL), 0.05),
        "w_in": nrm(ks[8], (DEPTH, D_MODEL, IN_COLS), D_MODEL ** -0.5),
        "sgu_ln_g": 1.0 + nrm(ks[9], (DEPTH, A_WIDTH), 0.05),
        "sgu_ln_b": nrm(ks[10], (DEPTH, A_WIDTH), 0.02),
        "w_s": nrm(ks[11], (DEPTH, A_GROUPS, CHUNK, CHUNK), CHUNK ** -0.5),
        "b_s": 1.0 + nrm(ks[12], (DEPTH, A_GROUPS, CHUNK), 0.05),
        "w_a_out": nrm(ks[13], (DEPTH, A_WIDTH, D_MODEL), A_WIDTH ** -0.5),
        "w_b_out": nrm(ks[14], (DEPTH, B_OUT_WIDTH, D_MODEL), B_OUT_WIDTH ** -0.5),
        "w_o": nrm(ks[15], (DEPTH, D_MODEL, D_MODEL), D_MODEL ** -0.5),
        "g_ffn": 1.0 + nrm(ks[16], (DEPTH, D_MODEL), 0.05),
        "peer_w_q": nrm(ks[17], (DEPTH, D_MODEL, PEER_HEADS * PEER_KEY_DIM), D_MODEL ** -0.5),
        "peer_sub_k1": nrm(ks[18], (DEPTH, N_KEYS, PEER_HALF), PEER_HALF ** -0.5),
        "peer_sub_k2": nrm(ks[19], (DEPTH, N_KEYS, PEER_HALF), PEER_HALF ** -0.5),
        "peer_u": nrm(ks[20], (DEPTH, N_EXPERTS, D_MODEL), D_MODEL ** -0.5),
        "peer_v": nrm(ks[21], (DEPTH, N_EXPERTS, D_MODEL), PEER_HEADS ** -0.5),
        "w_ple": nrm(ks[22], (DEPTH, PLE_DIM, D_MODEL), PLE_DIM ** -0.5),
        "w_ple_gate": nrm(ks[23], (DEPTH, D_MODEL, D_MODEL), D_MODEL ** -0.5),
        "g_final": 1.0 + nrm(ks[24], (D_MODEL,), 0.05),
    }


def reference(x_prompt, x_sample, cache_kv_w128, cache_kv_w512, cache_kv_w2048,
              p_prompt, p_sample, g_mix, w_in, sgu_ln_g, sgu_ln_b, w_s, b_s,
              w_a_out, w_b_out, w_o, g_ffn, peer_w_q, peer_sub_k1, peer_sub_k2,
              peer_u, peer_v, w_ple, w_ple_gate, g_final):
    caches = (cache_kv_w128, cache_kv_w512, cache_kv_w2048)
    s_len = x_prompt.shape[1]
    t_len = x_sample.shape[1]
    pos_p = jnp.arange(s_len, dtype=jnp.int32)
    pos_s = PAST_LEN + jnp.arange(t_len, dtype=jnp.int32)
    xp, xs = x_prompt, x_sample
    kv_p = ([], [], [])
    kv_s = ([], [], [])
    sgu_s = []
    for i in range(DEPTH):
        lw = (g_mix[i], w_in[i], sgu_ln_g[i], sgu_ln_b[i], w_s[i], b_s[i],
              w_a_out[i], w_b_out[i], w_o[i], g_ffn[i], peer_w_q[i], peer_sub_k1[i],
              peer_sub_k2[i], peer_u[i], peer_v[i], w_ple[i], w_ple_gate[i])

        def attend_prompt(gi, q, k, v):
            return dilated_group_prompt(q, k, v, DIL_GROUPS[gi][0], DIL_GROUPS[gi][1])

        def attend_sample(gi, q, k, v, layer_idx=i):
            c = caches[gi][layer_idx]
            return dilated_group_sample(q, k, v, c[:, :, 0], c[:, :, 1], DIL_GROUPS[gi][0], DIL_GROUPS[gi][1])

        xp, kp, vp, _ = layer(xp, p_prompt[i], pos_p, attend_prompt, *lw)
        xs, ksm, vsm, vn_s = layer(xs, p_sample[i], pos_s, attend_sample, *lw)
        for gi, (win, _) in enumerate(DIL_GROUPS):
            sl = slice(gi * HEADS_PER_GROUP, (gi + 1) * HEADS_PER_GROUP)
            rows = min(win, s_len)
            kv_p[gi].append(jnp.stack([kp[:, s_len - rows:, sl], vp[:, s_len - rows:, sl]], axis=2))
            kv_s[gi].append(jnp.stack([ksm[:, :, sl], vsm[:, :, sl]], axis=2))
        sgu_s.append(vn_s)
    y_prompt = rmsnorm(xp, g_final)
    y_sample = rmsnorm(xs, g_final)
    kv_w128_prompt = jnp.stack(kv_p[0], axis=0)
    kv_w512_prompt = jnp.stack(kv_p[1], axis=0)
    kv_w2048_prompt = jnp.stack(kv_p[2], axis=0)
    kv_w128_sample = jnp.stack(kv_s[0], axis=0)
    kv_w512_sample = jnp.stack(kv_s[1], axis=0)
    kv_w2048_sample = jnp.stack(kv_s[2], axis=0)
    sgu_v_sample = jnp.stack(sgu_s, axis=0)
    return (y_prompt, y_sample, kv_w128_prompt, kv_w512_prompt, kv_w2048_prompt,
            kv_w128_sample, kv_w512_sample, kv_w2048_sample, sgu_v_sample)
```

```python
import functools
import math

import jax
import jax.numpy as jnp
from jax import lax
from jax.experimental import pallas as pl
from jax.experimental.pallas import tpu as pltpu

F32 = jnp.float32
BF16 = jnp.bfloat16

D_MODEL = 2048
BATCH = 4
SEQ = 2048
DEC_BATCH = 128
PAST_LEN = 2048
EPS = 1e-6
CHUNK = 128
A_GROUPS = 8
A_WIDTH = 1024
HEAD_DIM = 128
HEADS_PER_GROUP = 4
DIL_GROUPS = ((128, 1), (512, 4), (2048, 16))
GROUP_WIDTH = HEADS_PER_GROUP * HEAD_DIM
B_WIDTH = 3 * GROUP_WIDTH
ATTN_SCALE = HEAD_DIM ** -0.5
ROPE_THETA = 500000.0
ROT_HALF = HEAD_DIM // 8
IN_COLS = 2 * A_WIDTH + 3 * B_WIDTH + 2 * D_MODEL
N_KEYS = 128
N_EXPERTS = N_KEYS * N_KEYS
PEER_HEADS = 8
PEER_HALF = 128
PEER_TOPK = 16
PLE_DIM = 256

N_PROMPT = BATCH * SEQ
N_TOK = N_PROMPT + DEC_BATCH
N_PAD = 8448
LANE = 128

PROJ_TN = 512
COL_GATE = 2 * A_WIDTH // PROJ_TN
COL_Q = COL_GATE + 2 * D_MODEL // PROJ_TN
COL_K = COL_Q + B_WIDTH // PROJ_TN
COL_V = COL_K + B_WIDTH // PROJ_TN
N_COL_BLOCKS = IN_COLS // PROJ_TN

VMEM_LIMIT = 56 * 1024 * 1024


def _params(sem, vmem=VMEM_LIMIT):
    return pltpu.CompilerParams(dimension_semantics=sem, vmem_limit_bytes=vmem)


def _const_spec(shape):
    nd = len(shape)
    return pl.BlockSpec(shape, lambda *_: (0,) * nd, pipeline_mode=pl.Buffered(1))


def _rms(x, g):
    r = lax.rsqrt(jnp.mean(x * x, axis=-1, keepdims=True) + EPS)
    return (x * r) * g


def _sigmoid(x):
    return 1.0 / (1.0 + jnp.exp(-x))


PROJ_TM = 768


def _proj_kernel(x_ref, g_ref, w_ref, c_ref, s1_ref, s2_ref, o_ref, h_ref):
    j = pl.program_id(1)

    @pl.when(j == 0)
    def _():
        h_ref[...] = _rms(x_ref[...], g_ref[...]).astype(BF16)

    acc = jnp.dot(h_ref[...], w_ref[...], preferred_element_type=F32)

    @pl.when(j < COL_GATE)
    def _():
        o_ref[...] = jax.nn.gelu(acc)

    @pl.when((j >= COL_GATE) & (j < COL_Q))
    def _():
        o_ref[...] = _sigmoid(acc)

    @pl.when((j >= COL_Q) & (j < COL_V))
    def _():
        c = c_ref[...]
        s1 = s1_ref[...]
        s2 = s2_ref[...]
        for hh in range(PROJ_TN // HEAD_DIM):
            a = acc[:, hh * HEAD_DIM:(hh + 1) * HEAD_DIM]
            o_ref[:, hh * HEAD_DIM:(hh + 1) * HEAD_DIM] = (
                a * c + pltpu.roll(a, HEAD_DIM - ROT_HALF, 1) * s1 + pltpu.roll(a, ROT_HALF, 1) * s2)

    @pl.when(j >= COL_V)
    def _():
        o_ref[...] = acc


def _proj(x, g, w, c, s1, s2):
    return pl.pallas_call(
        _proj_kernel,
        grid=(N_PAD // PROJ_TM, N_COL_BLOCKS),
        in_specs=[
            pl.BlockSpec((PROJ_TM, D_MODEL), lambda i, j: (i, 0)),
            pl.BlockSpec((1, D_MODEL), lambda i, j: (0, 0)),
            pl.BlockSpec((D_MODEL, PROJ_TN), lambda i, j: (0, j)),
            pl.BlockSpec((PROJ_TM, HEAD_DIM), lambda i, j: (i, 0)),
            pl.BlockSpec((PROJ_TM, HEAD_DIM), lambda i, j: (i, 0)),
            pl.BlockSpec((PROJ_TM, HEAD_DIM), lambda i, j: (i, 0)),
        ],
        out_specs=pl.BlockSpec((PROJ_TM, PROJ_TN), lambda i, j: (i, j)),
        out_shape=jax.ShapeDtypeStruct((N_PAD, IN_COLS), F32),
        scratch_shapes=[pltpu.VMEM((PROJ_TM, D_MODEL), BF16)],
        compiler_params=_params(("parallel", "arbitrary")),
        name="proj",
    )(x, g, w, c, s1, s2)


N_PROMPT_CHUNKS = N_PROMPT // CHUNK


def _sgu_kernel(u_ref, gv_ref, lng_ref, lnb_ref, ws_ref, bsb_ref, wd_ref, b0_ref, o_ref, vn_ref):
    i = pl.program_id(0)
    gv = gv_ref[...]
    mu = jnp.mean(gv, axis=-1, keepdims=True)
    var = jnp.mean(jnp.square(gv - mu), axis=-1, keepdims=True)
    vn = ((gv - mu) * lax.rsqrt(var + EPS)) * lng_ref[...] + lnb_ref[...]

    @pl.when(i < N_PROMPT_CHUNKS)
    def _():
        row = lax.broadcasted_iota(jnp.int32, (CHUNK, CHUNK), 0)
        col = lax.broadcasted_iota(jnp.int32, (CHUNK, CHUNK), 1)
        causal = col <= row
        for g in range(A_GROUPS):
            cs = slice(g * CHUNK, (g + 1) * CHUNK)
            w = jnp.where(causal, ws_ref[g], 0.0).astype(BF16)
            mixed = jnp.dot(w, vn[:, cs].astype(BF16), preferred_element_type=F32) + bsb_ref[g]
            o_ref[:, cs] = u_ref[:, cs] * mixed

    @pl.when(i >= N_PROMPT_CHUNKS)
    def _():
        o_ref[...] = u_ref[...] * (vn * wd_ref[...] + b0_ref[...])

    @pl.when(i == N_PROMPT_CHUNKS)
    def _():
        vn_ref[...] = vn


def _sgu(proj, lng, lnb, ws, bsb, wd, b0):
    return pl.pallas_call(
        _sgu_kernel,
        grid=(N_PAD // CHUNK,),
        in_specs=[
            pl.BlockSpec((CHUNK, A_WIDTH), lambda i: (i, 0)),
            pl.BlockSpec((CHUNK, A_WIDTH), lambda i: (i, 1)),
            pl.BlockSpec((1, A_WIDTH), lambda i: (0, 0)),
            pl.BlockSpec((1, A_WIDTH), lambda i: (0, 0)),
            pl.BlockSpec((A_GROUPS, CHUNK, CHUNK), lambda i: (0, 0, 0)),
            pl.BlockSpec((A_GROUPS, CHUNK, CHUNK), lambda i: (0, 0, 0)),
            pl.BlockSpec((1, A_WIDTH), lambda i: (0, 0)),
            pl.BlockSpec((1, A_WIDTH), lambda i: (0, 0)),
        ],
        out_specs=[
            pl.BlockSpec((CHUNK, A_WIDTH), lambda i: (i, 0)),
            pl.BlockSpec((DEC_BATCH, A_WIDTH), lambda i: (0, 0)),
        ],
        out_shape=[
            jax.ShapeDtypeStruct((N_PAD, A_WIDTH), F32),
            jax.ShapeDtypeStruct((DEC_BATCH, A_WIDTH), F32),
        ],
        compiler_params=_params(("arbitrary",)),
        name="sgu",
    )(proj, proj, lng, lnb, ws, bsb, wd, b0)


Q_BLOCK = 128


def _attn_prompt_kernel(q_ref, k_ref, v_ref, o_ref, m_ref, d_ref):
    g = pl.program_id(1)
    qb = pl.program_id(2)
    t0 = pl.multiple_of(qb * Q_BLOCK, Q_BLOCK)
    rows = pl.ds(t0, Q_BLOCK)

    def group(gi, window, dil):
        n_keys = min(window + Q_BLOCK, SEQ)
        start = pl.multiple_of(jnp.maximum(t0 - window, 0), Q_BLOCK)
        qpos = t0 + lax.broadcasted_iota(jnp.int32, (Q_BLOCK, n_keys), 0)
        kpos = start + lax.broadcasted_iota(jnp.int32, (Q_BLOCK, n_keys), 1)
        dist = qpos - kpos
        okf = jnp.where(dist >= 0, 1.0, 0.0) * jnp.where(dist <= window, 1.0, 0.0)
        okf = okf * jnp.where((dist & (dil - 1)) == 0, 1.0, 0.0)
        valid = okf > 0.5
        for h in range(HEADS_PER_GROUP):
            cs = slice(h * HEAD_DIM, (h + 1) * HEAD_DIM)
            q = q_ref[:, cs].astype(BF16)
            k = k_ref[pl.ds(start, n_keys), cs].astype(BF16)
            v = v_ref[pl.ds(start, n_keys), cs].astype(BF16)
            s = lax.dot_general(q, k, (((1,), (1,)), ((), ())), preferred_element_type=F32) * ATTN_SCALE
            s = jnp.where(valid, s, -jnp.inf)
            m = jnp.max(s, axis=-1, keepdims=True)
            e = jnp.exp(s - m)
            den = jnp.sum(e, axis=-1, keepdims=True)
            o = jnp.dot((e / den).astype(BF16), v, preferred_element_type=F32)
            lse = jnp.broadcast_to(m + jnp.log(den), (Q_BLOCK, HEAD_DIM))
            if gi == 0:
                o_ref[rows, cs] = o
                m_ref[rows, cs] = lse
                d_ref[rows, cs] = jnp.ones_like(lse)
            else:
                m_old = m_ref[rows, cs]
                m_new = jnp.maximum(m_old, lse)
                a = jnp.exp(m_old - m_new)
                b = jnp.exp(lse - m_new)
                acc = o_ref[rows, cs] * a + o * b
                den_c = d_ref[rows, cs] * a + b
                if gi == len(DIL_GROUPS) - 1:
                    o_ref[rows, cs] = acc / den_c
                else:
                    o_ref[rows, cs] = acc
                    m_ref[rows, cs] = m_new
                    d_ref[rows, cs] = den_c

    for gi, (window, dil) in enumerate(DIL_GROUPS):
        pl.when(g == gi)(functools.partial(group, gi, window, dil))


def _attn_prompt(proj):
    n_qb = SEQ // Q_BLOCK
    return pl.pallas_call(
        _attn_prompt_kernel,
        grid=(BATCH, len(DIL_GROUPS), n_qb),
        in_specs=[
            pl.BlockSpec((Q_BLOCK, GROUP_WIDTH), lambda b, g, qb: (b * n_qb + qb, COL_Q + g)),
            pl.BlockSpec((SEQ, GROUP_WIDTH), lambda b, g, qb: (b, COL_K + g)),
            pl.BlockSpec((SEQ, GROUP_WIDTH), lambda b, g, qb: (b, COL_V + g)),
        ],
        out_specs=pl.BlockSpec((SEQ, GROUP_WIDTH), lambda b, g, qb: (b, 0)),
        out_shape=jax.ShapeDtypeStruct((N_PROMPT, GROUP_WIDTH), F32),
        scratch_shapes=[pltpu.VMEM((SEQ, GROUP_WIDTH), F32), pltpu.VMEM((SEQ, GROUP_WIDTH), F32)],
        compiler_params=_params(("parallel", "arbitrary", "arbitrary")),
        name="attn_prompt",
    )(proj, proj, proj)


SAMPLE_NB = 4


def _attn_sample_kernel(q_ref, kn_ref, vn_ref, k0_ref, v0_ref, k1_ref, v1_ref, k2_ref, v2_ref, o_ref):
    caches = ((k0_ref, v0_ref), (k1_ref, v1_ref), (k2_ref, v2_ref))

    def seq(n, carry):
        outs = []
        lses = []
        for g, (kc_ref, vc_ref) in enumerate(caches):
            q = q_ref[n, g]
            kc = kc_ref[n]
            s = jnp.sum(kc * q[None], axis=-1, keepdims=True) * ATTN_SCALE
            sn = jnp.sum(kn_ref[n, g] * q, axis=-1, keepdims=True) * ATTN_SCALE
            m = jnp.maximum(jnp.max(s, axis=0), sn)
            e = jnp.exp(s - m[None])
            en = jnp.exp(sn - m)
            den = jnp.sum(e, axis=0) + en
            o = (jnp.sum(e * vc_ref[n], axis=0) + en * vn_ref[n, g]) / den
            outs.append(o)
            lses.append(m + jnp.log(den))
        mx = jnp.maximum(jnp.maximum(lses[0], lses[1]), lses[2])
        ws = [jnp.exp(l - mx) for l in lses]
        tot = ws[0] + ws[1] + ws[2]
        o_ref[n] = (ws[0] * outs[0] + ws[1] * outs[1] + ws[2] * outs[2]) / tot
        return carry

    lax.fori_loop(0, SAMPLE_NB, seq, 0)


def _attn_sample(qs, kns, vns, caches):
    nb = SAMPLE_NB
    small = pl.BlockSpec((nb, len(DIL_GROUPS), HEADS_PER_GROUP, HEAD_DIM), lambda n: (n, 0, 0, 0))
    in_specs = [small, small, small]
    args = [qs, kns, vns]
    for c in caches:
        for kv in (0, 1):
            in_specs.append(pl.BlockSpec(
                (nb, N_KEYS, None, None, HEADS_PER_GROUP, HEAD_DIM),
                functools.partial(lambda n, kv_: (n, 0, 0, kv_, 0, 0), kv_=kv)))
            args.append(c)
    return pl.pallas_call(
        _attn_sample_kernel,
        grid=(DEC_BATCH // nb,),
        in_specs=in_specs,
        out_specs=pl.BlockSpec((nb, HEADS_PER_GROUP, HEAD_DIM), lambda n: (n, 0, 0)),
        out_shape=jax.ShapeDtypeStruct((DEC_BATCH, HEADS_PER_GROUP, HEAD_DIM), F32),
        compiler_params=_params(("parallel",)),
        name="attn_sample",
    )(*args)


MERGE_TM = 384


def _merge_kernel(x_ref, a_ref, b_ref, ga_ref, gb_ref, wa_ref, wb_ref, wo_ref, o_ref):
    pa = jnp.dot(a_ref[...].astype(BF16), wa_ref[...], preferred_element_type=F32)
    pb = jnp.dot(b_ref[...].astype(BF16), wb_ref[...], preferred_element_type=F32)
    merged = ga_ref[...] * pa + gb_ref[...] * pb
    o_ref[...] = x_ref[...] + jnp.dot(merged.astype(BF16), wo_ref[...], preferred_element_type=F32)


def _merge(x, amix, bmix, proj, wa, wb, wo):
    tm = MERGE_TM
    gate_a_blk = COL_GATE * PROJ_TN // D_MODEL
    return pl.pallas_call(
        _merge_kernel,
        grid=(N_PAD // tm,),
        in_specs=[
            pl.BlockSpec((tm, D_MODEL), lambda i: (i, 0)),
            pl.BlockSpec((tm, A_WIDTH), lambda i: (i, 0)),
            pl.BlockSpec((tm, GROUP_WIDTH), lambda i: (i, 0)),
            pl.BlockSpec((tm, D_MODEL), lambda i: (i, gate_a_blk)),
            pl.BlockSpec((tm, D_MODEL), lambda i: (i, gate_a_blk + 1)),
            _const_spec((A_WIDTH, D_MODEL)),
            _const_spec((GROUP_WIDTH, D_MODEL)),
            _const_spec((D_MODEL, D_MODEL)),
        ],
        out_specs=pl.BlockSpec((tm, D_MODEL), lambda i: (i, 0)),
        out_shape=jax.ShapeDtypeStruct((N_PAD, D_MODEL), F32),
        compiler_params=_params(("parallel",)),
        name="merge",
    )(x, amix, bmix, proj, proj, wa, wb, wo)


GATE_TB = 128
PAIRS_PER_RANK = tuple(PEER_TOPK // (a + 1) for a in range(PEER_TOPK))


def _extract_sorted(v, count):
    n = v.shape[0]
    pos = lax.broadcasted_iota(jnp.int32, v.shape, 0)
    out = []
    for r in range(count):
        m = jnp.max(v, axis=0, keepdims=True)
        out.append(m)
        if r + 1 < count:
            first = jnp.min(jnp.where(v == m, pos, n), axis=0, keepdims=True)
            v = jnp.where(pos == first, -jnp.inf, v)
    return out


def _pair_grid(row1, col1, all1, all2):
    pieces = [(row1[0], all2[0:8], 8), (row1[0], all2[8:16], 8)]
    for a in range(1, 8):
        pieces.append((row1[a], all2[0:8], PAIRS_PER_RANK[a]))
    pieces.append((all1[8:16], col1, 8))
    return pieces


def _peer_gate_kernel(x_ref, g_ref, wq_ref, k1_ref, k2_ref,
                      xnt_ref, s1_ref, a_ref, s2_ref, bn_ref, tau_ref):
    xn = _rms(x_ref[...], g_ref[...])
    xnt_ref[...] = xn.T.astype(BF16)
    qp = jnp.dot(xn.astype(BF16), wq_ref[...], preferred_element_type=F32).astype(BF16)
    sub8 = lax.broadcasted_iota(jnp.int32, (8, GATE_TB), 0)
    nt = (((1,), (1,)), ((), ()))
    for h in range(PEER_HEADS):
        c0 = h * 2 * PEER_HALF
        s1 = lax.dot_general(k1_ref[...], qp[:, c0:c0 + PEER_HALF], nt, preferred_element_type=F32)
        s2 = lax.dot_general(k2_ref[...], qp[:, c0 + PEER_HALF:c0 + 2 * PEER_HALF], nt,
                             preferred_element_type=F32)
        t1 = _extract_sorted(s1, PEER_TOPK)
        t2 = _extract_sorted(s2, PEER_TOPK)
        t1_all = jnp.concatenate(t1, axis=0)
        t2_all = jnp.concatenate(t2, axis=0)
        sums = []
        for r1, r2, nvalid in _pair_grid(t1, t2[0], t1_all, t2_all):
            c = r1 + r2
            sums.append(c if nvalid == 8 else jnp.where(sub8 < nvalid, c, -jnp.inf))
        cand = jnp.concatenate(sums, axis=0)
        tau = _extract_sorted(cand, PEER_TOPK)[-1]
        e1 = jnp.exp(t1_all - t1[0])
        e2 = jnp.exp(t2_all - t2[0])
        e1_rows = [e1[a:a + 1] for a in range(8)]
        prods = [r1 * r2 for r1, r2, _ in _pair_grid(e1_rows, e2[0:1], e1, e2)]
        prod = jnp.concatenate(prods, axis=0)
        z = jnp.sum(jnp.where(cand >= tau, prod, 0.0), axis=0, keepdims=True)
        s1_ref[h] = s1
        s2_ref[h] = s2
        a_ref[h] = jnp.exp(s1 - t1[0])
        bn_ref[h] = jnp.exp(s2 - t2[0]) / z
        tau_ref[h] = tau


def _peer_gate(x1, g, wq, k1, k2):
    tb = GATE_TB
    big = pl.BlockSpec((PEER_HEADS, N_KEYS, tb), lambda i: (0, 0, i))
    big_shape = jax.ShapeDtypeStruct((PEER_HEADS, N_KEYS, N_PAD), F32)
    return pl.pallas_call(
        _peer_gate_kernel,
        grid=(N_PAD // tb,),
        in_specs=[
            pl.BlockSpec((tb, D_MODEL), lambda i: (i, 0)),
            pl.BlockSpec((1, D_MODEL), lambda i: (0, 0)),
            _const_spec((D_MODEL, PEER_HEADS * 2 * PEER_HALF)),
            _const_spec((N_KEYS, PEER_HALF)),
            _const_spec((N_KEYS, PEER_HALF)),
        ],
        out_specs=[
            pl.BlockSpec((D_MODEL, tb), lambda i: (0, i)),
            big, big, big, big,
            pl.BlockSpec((PEER_HEADS, 1, tb), lambda i: (0, 0, i)),
        ],
        out_shape=[
            jax.ShapeDtypeStruct((D_MODEL, N_PAD), BF16),
            big_shape, big_shape, big_shape, big_shape,
            jax.ShapeDtypeStruct((PEER_HEADS, 1, N_PAD), F32),
        ],
        compiler_params=_params(("parallel",)),
        name="peer_gate",
    )(x1, g, wq, k1, k2)


DENSE_TB = 768
DENSE_EB = 1024
DENSE_NI = DENSE_EB // N_KEYS


def _peer_dense_kernel(xnt_ref, s1_ref, a_ref, s2_ref, bn_ref, tau_ref, u_ref, vt_ref,
                       o_ref, st_ref, wt_ref):
    e = pl.program_id(1)

    @pl.when(e == 0)
    def _():
        o_ref[...] = jnp.zeros_like(o_ref)

    st_ref[...] = jnp.dot(u_ref[...], xnt_ref[...], preferred_element_type=F32)

    for ii in range(DENSE_NI):
        rows = slice(ii * N_KEYS, (ii + 1) * N_KEYS)
        for c in range(DENSE_TB // LANE):
            lanes = slice(c * LANE, (c + 1) * LANE)
            gate = jnp.zeros((N_KEYS, LANE), F32)
            for h in range(PEER_HEADS):
                s1_row = s1_ref[h, ii:ii + 1, lanes]
                a_row = a_ref[h, ii:ii + 1, lanes]
                hit = (s1_row + s2_ref[h, :, lanes]) >= tau_ref[h, :, lanes]
                gate = gate + jnp.where(hit, a_row * bn_ref[h, :, lanes], 0.0)
            act = jax.nn.gelu(st_ref[rows, lanes])
            wt_ref[rows, lanes] = (gate * act).astype(BF16)

    o_ref[...] += jnp.dot(vt_ref[...], wt_ref[...], preferred_element_type=F32)


def _peer_dense(xnt, s1t, at, s2t, bnt, tau, u_bf, vt_bf):
    tb, eb = DENSE_TB, DENSE_EB
    per_tok = lambda shape: pl.BlockSpec(shape, lambda t, e: (0,) * (len(shape) - 1) + (t,),
                                         pipeline_mode=pl.Buffered(1))
    by_key = pl.BlockSpec((PEER_HEADS, DENSE_NI, tb), lambda t, e: (0, e, t))
    return pl.pallas_call(
        _peer_dense_kernel,
        grid=(N_PAD // tb, N_EXPERTS // eb),
        in_specs=[
            per_tok((D_MODEL, tb)),
            by_key, by_key,
            per_tok((PEER_HEADS, N_KEYS, tb)),
            per_tok((PEER_HEADS, N_KEYS, tb)),
            per_tok((PEER_HEADS, 1, tb)),
            pl.BlockSpec((eb, D_MODEL), lambda t, e: (e, 0)),
            pl.BlockSpec((D_MODEL, eb), lambda t, e: (0, e)),
        ],
        out_specs=pl.BlockSpec((D_MODEL, tb), lambda t, e: (0, t)),
        out_shape=jax.ShapeDtypeStruct((D_MODEL, N_PAD), F32),
        scratch_shapes=[pltpu.VMEM((eb, tb), F32), pltpu.VMEM((eb, tb), BF16)],
        compiler_params=_params(("parallel", "arbitrary")),
        name="peer_dense",
    )(xnt, s1t, at, s2t, bnt, tau, u_bf, vt_bf)


FINAL_TM = 384


def _final_kernel(x_ref, pt_ref, p_ref, wg_ref, wp_ref, g_ref, o_ref):
    x2 = x_ref[...] + pt_ref[...].T
    gate = _sigmoid(jnp.dot(x2.astype(BF16), wg_ref[...], preferred_element_type=F32))
    emb = jnp.dot(p_ref[...].astype(BF16), wp_ref[...], preferred_element_type=F32)
    o_ref[...] = _rms(x2 + gate * emb, g_ref[...])


def _final(x1, peer_t, p, wg, wp, g):
    tm = FINAL_TM
    return pl.pallas_call(
        _final_kernel,
        grid=(N_PAD // tm,),
        in_specs=[
            pl.BlockSpec((tm, D_MODEL), lambda i: (i, 0)),
            pl.BlockSpec((D_MODEL, tm), lambda i: (0, i)),
            pl.BlockSpec((tm, PLE_DIM), lambda i: (i, 0)),
            _const_spec((D_MODEL, D_MODEL)),
            _const_spec((PLE_DIM, D_MODEL)),
            pl.BlockSpec((1, D_MODEL), lambda i: (0, 0)),
        ],
        out_specs=pl.BlockSpec((tm, D_MODEL), lambda i: (i, 0)),
        out_shape=jax.ShapeDtypeStruct((N_PAD, D_MODEL), F32),
        compiler_params=_params(("parallel",)),
        name="final",
    )(x1, peer_t, p, wg, wp, g)


def _rope_tables():
    pos = jnp.concatenate([
        jnp.tile(jnp.arange(SEQ, dtype=jnp.int32), BATCH),
        jnp.full((DEC_BATCH,), PAST_LEN, jnp.int32),
        jnp.zeros((N_PAD - N_TOK,), jnp.int32)])
    inv = ROPE_THETA ** (-jnp.arange(ROT_HALF, dtype=F32) / ROT_HALF)
    ang = pos.astype(F32)[:, None] * inv[None, :]
    cos, sin = jnp.cos(ang), jnp.sin(ang)
    n = pos.shape[0]
    zeros = lambda w: jnp.zeros((n, w), F32)
    c = jnp.concatenate([cos, cos, jnp.ones((n, HEAD_DIM - 2 * ROT_HALF), F32)], axis=1)
    s_hi = jnp.concatenate([-sin, zeros(HEAD_DIM - ROT_HALF)], axis=1)
    s_lo = jnp.concatenate([zeros(ROT_HALF), sin, zeros(HEAD_DIM - 2 * ROT_HALF)], axis=1)
    return c, s_hi, s_lo


def kernel(x_prompt, x_sample, cache_kv_w128, cache_kv_w512, cache_kv_w2048, p_prompt, p_sample, g_mix, w_in, sgu_ln_g, sgu_ln_b, w_s, b_s, w_a_out, w_b_out, w_o, g_ffn, peer_w_q, peer_sub_k1, peer_sub_k2, peer_u, peer_v, w_ple, w_ple_gate, g_final):
    assert x_prompt.shape == (BATCH, SEQ, D_MODEL) and x_sample.shape == (DEC_BATCH, 1, D_MODEL)
    assert w_in.shape == (1, D_MODEL, IN_COLS)
    pad = N_PAD - N_TOK
    x = jnp.concatenate([x_prompt.reshape(N_PROMPT, D_MODEL), x_sample.reshape(DEC_BATCH, D_MODEL),
                         jnp.zeros((pad, D_MODEL), F32)], axis=0)
    p = jnp.concatenate([p_prompt.reshape(N_PROMPT, PLE_DIM), p_sample.reshape(DEC_BATCH, PLE_DIM),
                         jnp.zeros((pad, PLE_DIM), F32)], axis=0)

    w = w_in[0]
    o_q = 2 * A_WIDTH
    o_g = o_q + 3 * B_WIDTH
    w_perm = jnp.concatenate([w[:, :o_q], w[:, o_g:], w[:, o_q:o_g]], axis=1).astype(BF16)
    c, s_hi, s_lo = _rope_tables()
    proj = _proj(x, g_mix, w_perm, c, s_hi, s_lo)

    bsb = jnp.broadcast_to(b_s[0][:, :, None], (A_GROUPS, CHUNK, CHUNK))
    wd = jnp.repeat(w_s[0, :, 0, 0], CHUNK)[None]
    b0 = jnp.repeat(b_s[0, :, 0], CHUNK)[None]
    amix, vn_s = _sgu(proj, sgu_ln_g, sgu_ln_b, w_s[0], bsb, wd, b0)

    bmix_p = _attn_prompt(proj)
    cq, ck, cv = COL_Q * PROJ_TN, COL_K * PROJ_TN, COL_V * PROJ_TN
    srows = proj[N_PROMPT:N_TOK]
    per_head = lambda a: a.reshape(DEC_BATCH, len(DIL_GROUPS), HEADS_PER_GROUP, HEAD_DIM)
    caches = []
    for cache, (window, dil) in zip((cache_kv_w128, cache_kv_w512, cache_kv_w2048), DIL_GROUPS):
        l_buf = cache.shape[2]
        assert l_buf == N_KEYS * dil
        caches.append(cache.reshape(DEC_BATCH, N_KEYS, dil, 2, HEADS_PER_GROUP, HEAD_DIM))
    bmix_s = _attn_sample(per_head(srows[:, cq:ck]), per_head(srows[:, ck:cv]), per_head(srows[:, cv:]), caches)
    bmix = jnp.concatenate([bmix_p, bmix_s.reshape(DEC_BATCH, GROUP_WIDTH),
                            jnp.zeros((pad, GROUP_WIDTH), F32)], axis=0)

    x1 = _merge(x, amix, bmix, proj, w_a_out[0].astype(BF16), w_b_out[0].astype(BF16), w_o[0].astype(BF16))

    xnt, s1t, at, s2t, bnt, tau = _peer_gate(x1, g_ffn, peer_w_q[0].astype(BF16),
                                             peer_sub_k1[0].astype(BF16), peer_sub_k2[0].astype(BF16))
    peer_t = _peer_dense(xnt, s1t, at, s2t, bnt, tau, peer_u[0].astype(BF16), peer_v[0].T.astype(BF16))

    y = _final(x1, peer_t, p, w_ple_gate[0].astype(BF16), w_ple[0].astype(BF16), g_final[None])

    y_prompt = y[:N_PROMPT].reshape(BATCH, SEQ, D_MODEL)
    y_sample = y[N_PROMPT:N_TOK].reshape(DEC_BATCH, 1, D_MODEL)
    n_heads = len(DIL_GROUPS) * HEADS_PER_GROUP
    kp = proj[:N_PROMPT, ck:cv].reshape(BATCH, SEQ, n_heads, HEAD_DIM)
    vp = proj[:N_PROMPT, cv:].reshape(BATCH, SEQ, n_heads, HEAD_DIM)
    ks = srows[:, ck:cv].reshape(DEC_BATCH, 1, n_heads, HEAD_DIM)
    vs = srows[:, cv:].reshape(DEC_BATCH, 1, n_heads, HEAD_DIM)
    kv_p, kv_s = [], []
    for gi, (window, _) in enumerate(DIL_GROUPS):
        sl = slice(gi * HEADS_PER_GROUP, (gi + 1) * HEADS_PER_GROUP)
        rows = min(window, SEQ)
        kv_p.append(jnp.stack([kp[:, SEQ - rows:, sl], vp[:, SEQ - rows:, sl]], axis=2)[None])
        kv_s.append(jnp.stack([ks[:, :, sl], vs[:, :, sl]], axis=2)[None])
    sgu_v_sample = vn_s.reshape(1, DEC_BATCH, 1, A_WIDTH)
    return (y_prompt, y_sample, kv_p[0], kv_p[1], kv_p[2], kv_s[0], kv_s[1], kv_s[2], sgu_v_sample)
```

```python
import functools
import math

import jax
import jax.numpy as jnp
from jax import lax
from jax.experimental import pallas as pl
from jax.experimental.pallas import tpu as pltpu

F32 = jnp.float32
BF16 = jnp.bfloat16

D_MODEL = 2048
BATCH = 4
SEQ = 2048
DEC_BATCH = 128
PAST_LEN = 2048
EPS = 1e-6
CHUNK = 128
A_GROUPS = 8
A_WIDTH = 1024
HEAD_DIM = 128
HEADS_PER_GROUP = 4
DIL_GROUPS = ((128, 1), (512, 4), (2048, 16))
GROUP_WIDTH = HEADS_PER_GROUP * HEAD_DIM
B_WIDTH = 3 * GROUP_WIDTH
ATTN_SCALE = HEAD_DIM ** -0.5
ROPE_THETA = 500000.0
ROT_HALF = HEAD_DIM // 8
IN_COLS = 2 * A_WIDTH + 3 * B_WIDTH + 2 * D_MODEL
N_KEYS = 128
N_EXPERTS = N_KEYS * N_KEYS
PEER_HEADS = 8
PEER_HALF = 128
PEER_TOPK = 16
PLE_DIM = 256

N_PROMPT = BATCH * SEQ
N_TOK = N_PROMPT + DEC_BATCH
N_PAD = 8448
LANE = 128

PROJ_TN = 512
COL_GATE = 2 * A_WIDTH // PROJ_TN
COL_Q = COL_GATE + 2 * D_MODEL // PROJ_TN
COL_K = COL_Q + B_WIDTH // PROJ_TN
COL_V = COL_K + B_WIDTH // PROJ_TN
N_COL_BLOCKS = IN_COLS // PROJ_TN

VMEM_LIMIT = 56 * 1024 * 1024


def _params(sem, vmem=VMEM_LIMIT):
    return pltpu.CompilerParams(dimension_semantics=sem, vmem_limit_bytes=vmem)


def _const_spec(shape):
    nd = len(shape)
    return pl.BlockSpec(shape, lambda *_: (0,) * nd, pipeline_mode=pl.Buffered(1))


def _rms(x, g):
    r = lax.rsqrt(jnp.mean(x * x, axis=-1, keepdims=True) + EPS)
    return (x * r) * g


def _sigmoid(x):
    return 1.0 / (1.0 + jnp.exp(-x))


PROJ_TM = 768


def _proj_kernel(x_ref, g_ref, w_ref, c_ref, s1_ref, s2_ref, o_ref, h_ref):
    j = pl.program_id(1)

    @pl.when(j == 0)
    def _():
        h_ref[...] = _rms(x_ref[...], g_ref[...]).astype(BF16)

    acc = jnp.dot(h_ref[...], w_ref[...], preferred_element_type=F32)

    @pl.when(j < COL_GATE)
    def _():
        o_ref[...] = jax.nn.gelu(acc)

    @pl.when((j >= COL_GATE) & (j < COL_Q))
    def _():
        o_ref[...] = _sigmoid(acc)

    @pl.when((j >= COL_Q) & (j < COL_V))
    def _():
        c = c_ref[...]
        s1 = s1_ref[...]
        s2 = s2_ref[...]
        for hh in range(PROJ_TN // HEAD_DIM):
            a = acc[:, hh * HEAD_DIM:(hh + 1) * HEAD_DIM]
            o_ref[:, hh * HEAD_DIM:(hh + 1) * HEAD_DIM] = (
                a * c + pltpu.roll(a, HEAD_DIM - ROT_HALF, 1) * s1 + pltpu.roll(a, ROT_HALF, 1) * s2)

    @pl.when(j >= COL_V)
    def _():
        o_ref[...] = acc


def _w_in_col_block(j):
    n_qkv = 3 * B_WIDTH // PROJ_TN
    n_gate = 2 * D_MODEL // PROJ_TN
    return jnp.where(j < COL_GATE, j, jnp.where(j < COL_Q, j + n_qkv, j - n_gate))


def _proj(x, g, w, c, s1, s2):
    return pl.pallas_call(
        _proj_kernel,
        grid=(N_PAD // PROJ_TM, N_COL_BLOCKS),
        in_specs=[
            pl.BlockSpec((PROJ_TM, D_MODEL), lambda i, j: (i, 0)),
            pl.BlockSpec((1, D_MODEL), lambda i, j: (0, 0)),
            pl.BlockSpec((D_MODEL, PROJ_TN), lambda i, j: (0, _w_in_col_block(j))),
            pl.BlockSpec((PROJ_TM, HEAD_DIM), lambda i, j: (i, 0)),
            pl.BlockSpec((PROJ_TM, HEAD_DIM), lambda i, j: (i, 0)),
            pl.BlockSpec((PROJ_TM, HEAD_DIM), lambda i, j: (i, 0)),
        ],
        out_specs=pl.BlockSpec((PROJ_TM, PROJ_TN), lambda i, j: (i, j)),
        out_shape=jax.ShapeDtypeStruct((N_PAD, IN_COLS), F32),
        scratch_shapes=[pltpu.VMEM((PROJ_TM, D_MODEL), BF16)],
        compiler_params=_params(("parallel", "arbitrary")),
        name="proj",
    )(x, g, w, c, s1, s2)


N_PROMPT_CHUNKS = N_PROMPT // CHUNK


def _sgu_kernel(u_ref, gv_ref, lng_ref, lnb_ref, ws_ref, bsb_ref, wd_ref, b0_ref, o_ref, vn_ref):
    i = pl.program_id(0)
    gv = gv_ref[...]
    mu = jnp.mean(gv, axis=-1, keepdims=True)
    var = jnp.mean(jnp.square(gv - mu), axis=-1, keepdims=True)
    vn = ((gv - mu) * lax.rsqrt(var + EPS)) * lng_ref[...] + lnb_ref[...]

    @pl.when(i < N_PROMPT_CHUNKS)
    def _():
        row = lax.broadcasted_iota(jnp.int32, (CHUNK, CHUNK), 0)
        col = lax.broadcasted_iota(jnp.int32, (CHUNK, CHUNK), 1)
        causal = col <= row
        for g in range(A_GROUPS):
            cs = slice(g * CHUNK, (g + 1) * CHUNK)
            w = jnp.where(causal, ws_ref[g], 0.0).astype(BF16)
            mixed = jnp.dot(w, vn[:, cs].astype(BF16), preferred_element_type=F32) + bsb_ref[g]
            o_ref[:, cs] = u_ref[:, cs] * mixed

    @pl.when(i >= N_PROMPT_CHUNKS)
    def _():
        o_ref[...] = u_ref[...] * (vn * wd_ref[...] + b0_ref[...])

    @pl.when(i == N_PROMPT_CHUNKS)
    def _():
        vn_ref[...] = vn


def _sgu(proj, lng, lnb, ws, bsb, wd, b0):
    return pl.pallas_call(
        _sgu_kernel,
        grid=(N_PAD // CHUNK,),
        in_specs=[
            pl.BlockSpec((CHUNK, A_WIDTH), lambda i: (i, 0)),
            pl.BlockSpec((CHUNK, A_WIDTH), lambda i: (i, 1)),
            pl.BlockSpec((1, A_WIDTH), lambda i: (0, 0)),
            pl.BlockSpec((1, A_WIDTH), lambda i: (0, 0)),
            pl.BlockSpec((A_GROUPS, CHUNK, CHUNK), lambda i: (0, 0, 0)),
            pl.BlockSpec((A_GROUPS, CHUNK, CHUNK), lambda i: (0, 0, 0)),
            pl.BlockSpec((1, A_WIDTH), lambda i: (0, 0)),
            pl.BlockSpec((1, A_WIDTH), lambda i: (0, 0)),
        ],
        out_specs=[
            pl.BlockSpec((CHUNK, A_WIDTH), lambda i: (i, 0)),
            pl.BlockSpec((DEC_BATCH, A_WIDTH), lambda i: (0, 0)),
        ],
        out_shape=[
            jax.ShapeDtypeStruct((N_PAD, A_WIDTH), F32),
            jax.ShapeDtypeStruct((DEC_BATCH, A_WIDTH), F32),
        ],
        compiler_params=_params(("arbitrary",)),
        name="sgu",
    )(proj, proj, lng, lnb, ws, bsb, wd, b0)


Q_BLOCK = 128


def _attn_prompt_kernel(q_ref, k_ref, v_ref, o_ref, m_ref, d_ref):
    g = pl.program_id(1)
    qb = pl.program_id(2)
    t0 = pl.multiple_of(qb * Q_BLOCK, Q_BLOCK)
    rows = pl.ds(t0, Q_BLOCK)

    def group(gi, window, dil):
        n_keys = min(window + Q_BLOCK, SEQ)
        start = pl.multiple_of(jnp.maximum(t0 - window, 0), Q_BLOCK)
        qpos = t0 + lax.broadcasted_iota(jnp.int32, (Q_BLOCK, n_keys), 0)
        kpos = start + lax.broadcasted_iota(jnp.int32, (Q_BLOCK, n_keys), 1)
        dist = qpos - kpos
        okf = jnp.where(dist >= 0, 1.0, 0.0) * jnp.where(dist <= window, 1.0, 0.0)
        okf = okf * jnp.where((dist & (dil - 1)) == 0, 1.0, 0.0)
        valid = okf > 0.5
        for h in range(HEADS_PER_GROUP):
            cs = slice(h * HEAD_DIM, (h + 1) * HEAD_DIM)
            q = q_ref[:, cs].astype(BF16)
            k = k_ref[pl.ds(start, n_keys), cs].astype(BF16)
            v = v_ref[pl.ds(start, n_keys), cs].astype(BF16)
            s = lax.dot_general(q, k, (((1,), (1,)), ((), ())), preferred_element_type=F32) * ATTN_SCALE
            s = jnp.where(valid, s, -jnp.inf)
            m = jnp.max(s, axis=-1, keepdims=True)
            e = jnp.exp(s - m)
            den = jnp.sum(e, axis=-1, keepdims=True)
            o = jnp.dot((e / den).astype(BF16), v, preferred_element_type=F32)
            lse = jnp.broadcast_to(m + jnp.log(den), (Q_BLOCK, HEAD_DIM))
            if gi == 0:
                o_ref[rows, cs] = o
                m_ref[rows, cs] = lse
                d_ref[rows, cs] = jnp.ones_like(lse)
            else:
                m_old = m_ref[rows, cs]
                m_new = jnp.maximum(m_old, lse)
                a = jnp.exp(m_old - m_new)
                b = jnp.exp(lse - m_new)
                acc = o_ref[rows, cs] * a + o * b
                den_c = d_ref[rows, cs] * a + b
                if gi == len(DIL_GROUPS) - 1:
                    o_ref[rows, cs] = acc / den_c
                else:
                    o_ref[rows, cs] = acc
                    m_ref[rows, cs] = m_new
                    d_ref[rows, cs] = den_c

    for gi, (window, dil) in enumerate(DIL_GROUPS):
        pl.when(g == gi)(functools.partial(group, gi, window, dil))


def _attn_prompt(proj):
    n_qb = SEQ // Q_BLOCK
    return pl.pallas_call(
        _attn_prompt_kernel,
        grid=(BATCH, len(DIL_GROUPS), n_qb),
        in_specs=[
            pl.BlockSpec((Q_BLOCK, GROUP_WIDTH), lambda b, g, qb: (b * n_qb + qb, COL_Q + g)),
            pl.BlockSpec((SEQ, GROUP_WIDTH), lambda b, g, qb: (b, COL_K + g)),
            pl.BlockSpec((SEQ, GROUP_WIDTH), lambda b, g, qb: (b, COL_V + g)),
        ],
        out_specs=pl.BlockSpec((SEQ, GROUP_WIDTH), lambda b, g, qb: (b, 0)),
        out_shape=jax.ShapeDtypeStruct((N_PROMPT, GROUP_WIDTH), F32),
        scratch_shapes=[pltpu.VMEM((SEQ, GROUP_WIDTH), F32), pltpu.VMEM((SEQ, GROUP_WIDTH), F32)],
        compiler_params=_params(("parallel", "arbitrary", "arbitrary")),
        name="attn_prompt",
    )(proj, proj, proj)


SAMPLE_NB = 4


def _attn_sample_kernel(q_ref, kn_ref, vn_ref, k0_ref, v0_ref, k1_ref, v1_ref, k2_ref, v2_ref, o_ref):
    caches = ((k0_ref, v0_ref), (k1_ref, v1_ref), (k2_ref, v2_ref))

    def seq(n, carry):
        outs = []
        lses = []
        for g, (kc_ref, vc_ref) in enumerate(caches):
            q = q_ref[n, g]
            kc = kc_ref[n]
            s = jnp.sum(kc * q[None], axis=-1, keepdims=True) * ATTN_SCALE
            sn = jnp.sum(kn_ref[n, g] * q, axis=-1, keepdims=True) * ATTN_SCALE
            m = jnp.maximum(jnp.max(s, axis=0), sn)
            e = jnp.exp(s - m[None])
            en = jnp.exp(sn - m)
            den = jnp.sum(e, axis=0) + en
            o = (jnp.sum(e * vc_ref[n], axis=0) + en * vn_ref[n, g]) / den
            outs.append(o)
            lses.append(m + jnp.log(den))
        mx = jnp.maximum(jnp.maximum(lses[0], lses[1]), lses[2])
        ws = [jnp.exp(l - mx) for l in lses]
        tot = ws[0] + ws[1] + ws[2]
        o_ref[n] = (ws[0] * outs[0] + ws[1] * outs[1] + ws[2] * outs[2]) / tot
        return carry

    lax.fori_loop(0, SAMPLE_NB, seq, 0)


def _attn_sample(qs, kns, vns, caches):
    nb = SAMPLE_NB
    small = pl.BlockSpec((nb, len(DIL_GROUPS), HEADS_PER_GROUP, HEAD_DIM), lambda n: (n, 0, 0, 0))
    in_specs = [small, small, small]
    args = [qs, kns, vns]
    for c in caches:
        for kv in (0, 1):
            in_specs.append(pl.BlockSpec(
                (nb, N_KEYS, None, None, HEADS_PER_GROUP, HEAD_DIM),
                functools.partial(lambda n, kv_: (n, 0, 0, kv_, 0, 0), kv_=kv)))
            args.append(c)
    return pl.pallas_call(
        _attn_sample_kernel,
        grid=(DEC_BATCH // nb,),
        in_specs=in_specs,
        out_specs=pl.BlockSpec((nb, HEADS_PER_GROUP, HEAD_DIM), lambda n: (n, 0, 0)),
        out_shape=jax.ShapeDtypeStruct((DEC_BATCH, HEADS_PER_GROUP, HEAD_DIM), F32),
        compiler_params=_params(("parallel",)),
        name="attn_sample",
    )(*args)


MERGE_TM = 384


def _merge_kernel(x_ref, a_ref, b_ref, ga_ref, gb_ref, wa_ref, wb_ref, wo_ref, o_ref):
    pa = jnp.dot(a_ref[...].astype(BF16), wa_ref[...], preferred_element_type=F32)
    pb = jnp.dot(b_ref[...].astype(BF16), wb_ref[...], preferred_element_type=F32)
    merged = ga_ref[...] * pa + gb_ref[...] * pb
    o_ref[...] = x_ref[...] + jnp.dot(merged.astype(BF16), wo_ref[...], preferred_element_type=F32)


def _merge(x, amix, bmix, proj, wa, wb, wo):
    tm = MERGE_TM
    gate_a_blk = COL_GATE * PROJ_TN // D_MODEL
    return pl.pallas_call(
        _merge_kernel,
        grid=(N_PAD // tm,),
        in_specs=[
            pl.BlockSpec((tm, D_MODEL), lambda i: (i, 0)),
            pl.BlockSpec((tm, A_WIDTH), lambda i: (i, 0)),
            pl.BlockSpec((tm, GROUP_WIDTH), lambda i: (i, 0)),
            pl.BlockSpec((tm, D_MODEL), lambda i: (i, gate_a_blk)),
            pl.BlockSpec((tm, D_MODEL), lambda i: (i, gate_a_blk + 1)),
            _const_spec((A_WIDTH, D_MODEL)),
            _const_spec((GROUP_WIDTH, D_MODEL)),
            _const_spec((D_MODEL, D_MODEL)),
        ],
        out_specs=pl.BlockSpec((tm, D_MODEL), lambda i: (i, 0)),
        out_shape=jax.ShapeDtypeStruct((N_PAD, D_MODEL), F32),
        compiler_params=_params(("parallel",)),
        name="merge",
    )(x, amix, bmix, proj, proj, wa, wb, wo)


GATE_TB = 128
PAIRS_PER_RANK = tuple(PEER_TOPK // (a + 1) for a in range(PEER_TOPK))


SUBLANES = 8


def _batcher_pairs(n):
    pairs = []

    def merge(lo, m, r):
        step = 2 * r
        if step < m:
            merge(lo, m, step)
            merge(lo + r, m, step)
            pairs.extend((i, i + r) for i in range(lo + r, lo + m - r, step))
        else:
            pairs.append((lo, lo + r))

    def sort(lo, m):
        if m > 1:
            sort(lo, m // 2)
            sort(lo + m // 2, m // 2)
            merge(lo, m, 1)

    sort(0, n)
    return tuple(pairs)


SORT16_PAIRS = _batcher_pairs(PEER_TOPK)


def _exchange(v, i, j):
    v[i], v[j] = jnp.maximum(v[i], v[j]), jnp.minimum(v[i], v[j])


def _bitonic_sort_desc(c):
    c = list(c)
    d = PEER_TOPK // 2
    while d:
        for k in range(PEER_TOPK):
            if not k & d:
                _exchange(c, k, k + d)
        d //= 2
    return c


def _merge_sublanes(v, sort_last):
    for shift in (4, 2, 1):
        other = [pltpu.roll(x, shift, 0) for x in v]
        c = [jnp.maximum(v[k], other[PEER_TOPK - 1 - k]) for k in range(PEER_TOPK)]
        v = c if (shift == 1 and not sort_last) else _bitonic_sort_desc(c)
    return v


def _top16_sorted(s):
    v = [s[k * SUBLANES:(k + 1) * SUBLANES] for k in range(s.shape[0] // SUBLANES)]
    for i, j in SORT16_PAIRS:
        _exchange(v, i, j)
    return _merge_sublanes(v, True)


def _by_sublane(rows, sub8):
    out = rows[SUBLANES - 1]
    for b in range(SUBLANES - 2, -1, -1):
        out = jnp.where(sub8 == b, rows[b], out)
    return out


def _peer_gate_kernel(x_ref, g_ref, wq_ref, k1_ref, k2_ref,
                      xnt_ref, s1_ref, a_ref, s2_ref, bn_ref, tau_ref):
    xn = _rms(x_ref[...], g_ref[...])
    xnt_ref[...] = xn.T.astype(BF16)
    qp = jnp.dot(xn.astype(BF16), wq_ref[...], preferred_element_type=F32).astype(BF16)
    sub8 = lax.broadcasted_iota(jnp.int32, (8, GATE_TB), 0)
    nt = (((1,), (1,)), ((), ()))
    for h in range(PEER_HEADS):
        c0 = h * 2 * PEER_HALF
        s1 = lax.dot_general(k1_ref[...], qp[:, c0:c0 + PEER_HALF], nt, preferred_element_type=F32)
        s2 = lax.dot_general(k2_ref[...], qp[:, c0 + PEER_HALF:c0 + 2 * PEER_HALF], nt,
                             preferred_element_type=F32)
        t1 = _top16_sorted(s1)
        t2 = _top16_sorted(s2)
        t2_lo = _by_sublane(t2[:SUBLANES], sub8)
        t2_hi = _by_sublane(t2[SUBLANES:], sub8)
        lo = []
        for a in range(PEER_TOPK):
            nvalid = min(SUBLANES, PAIRS_PER_RANK[a])
            c = t1[a] + t2_lo
            lo.append(c if nvalid == SUBLANES else jnp.where(sub8 < nvalid, c, -jnp.inf))
        hi = t1[0] + t2_hi
        ins = [jnp.maximum(lo[0], hi)]
        ins += [jnp.maximum(lo[k], jnp.minimum(lo[k - 1], hi)) for k in range(1, PEER_TOPK)]
        best = _merge_sublanes(ins, False)
        tau = functools.reduce(jnp.minimum, best)
        e2_lo = jnp.exp(t2_lo - t2[0])
        zacc = jnp.where(hi >= tau, jnp.exp(t2_hi - t2[0]), 0.0)
        for a in range(PEER_TOPK):
            zacc = zacc + jnp.where(lo[a] >= tau, jnp.exp(t1[a] - t1[0]) * e2_lo, 0.0)
        z = jnp.sum(zacc, axis=0, keepdims=True)
        s1_ref[h] = s1
        s2_ref[h] = s2
        a_ref[h] = jnp.exp(s1 - t1[0][0:1])
        bn_ref[h] = jnp.exp(s2 - t2[0][0:1]) / z
        tau_ref[h] = tau[0:1]


def _peer_gate(x1, g, wq, k1, k2):
    tb = GATE_TB
    big = pl.BlockSpec((PEER_HEADS, N_KEYS, tb), lambda i: (0, 0, i))
    big_shape = jax.ShapeDtypeStruct((PEER_HEADS, N_KEYS, N_PAD), F32)
    return pl.pallas_call(
        _peer_gate_kernel,
        grid=(N_PAD // tb,),
        in_specs=[
            pl.BlockSpec((tb, D_MODEL), lambda i: (i, 0)),
            pl.BlockSpec((1, D_MODEL), lambda i: (0, 0)),
            _const_spec((D_MODEL, PEER_HEADS * 2 * PEER_HALF)),
            _const_spec((N_KEYS, PEER_HALF)),
            _const_spec((N_KEYS, PEER_HALF)),
        ],
        out_specs=[
            pl.BlockSpec((D_MODEL, tb), lambda i: (0, i)),
            big, big, big, big,
            pl.BlockSpec((PEER_HEADS, 1, tb), lambda i: (0, 0, i)),
        ],
        out_shape=[
            jax.ShapeDtypeStruct((D_MODEL, N_PAD), BF16),
            big_shape, big_shape, big_shape, big_shape,
            jax.ShapeDtypeStruct((PEER_HEADS, 1, N_PAD), F32),
        ],
        compiler_params=_params(("parallel",)),
        name="peer_gate",
    )(x1, g, wq, k1, k2)


DENSE_TB = 768
DENSE_EB = 1024
DENSE_NI = DENSE_EB // N_KEYS


def _peer_dense_kernel(xnt_ref, s1_ref, a_ref, s2_ref, bn_ref, tau_ref, u_ref, vt_ref,
                       o_ref, st_ref, wt_ref):
    e = pl.program_id(1)

    @pl.when(e == 0)
    def _():
        o_ref[...] = jnp.zeros_like(o_ref)

    st_ref[...] = jnp.dot(u_ref[...], xnt_ref[...], preferred_element_type=F32)

    for ii in range(DENSE_NI):
        rows = slice(ii * N_KEYS, (ii + 1) * N_KEYS)
        for c in range(DENSE_TB // LANE):
            lanes = slice(c * LANE, (c + 1) * LANE)
            gate = jnp.zeros((N_KEYS, LANE), F32)
            for h in range(PEER_HEADS):
                s1_row = s1_ref[h, ii:ii + 1, lanes]
                a_row = a_ref[h, ii:ii + 1, lanes]
                hit = (s1_row + s2_ref[h, :, lanes]) >= tau_ref[h, :, lanes]
                gate = gate + jnp.where(hit, a_row * bn_ref[h, :, lanes], 0.0)
            act = jax.nn.gelu(st_ref[rows, lanes])
            wt_ref[rows, lanes] = (gate * act).astype(BF16)

    o_ref[...] += jnp.dot(vt_ref[...], wt_ref[...], preferred_element_type=F32)


def _peer_dense(xnt, s1t, at, s2t, bnt, tau, u_bf, vt_bf):
    tb, eb = DENSE_TB, DENSE_EB
    per_tok = lambda shape: pl.BlockSpec(shape, lambda t, e: (0,) * (len(shape) - 1) + (t,),
                                         pipeline_mode=pl.Buffered(1))
    by_key = pl.BlockSpec((PEER_HEADS, DENSE_NI, tb), lambda t, e: (0, e, t))
    return pl.pallas_call(
        _peer_dense_kernel,
        grid=(N_PAD // tb, N_EXPERTS // eb),
        in_specs=[
            per_tok((D_MODEL, tb)),
            by_key, by_key,
            per_tok((PEER_HEADS, N_KEYS, tb)),
            per_tok((PEER_HEADS, N_KEYS, tb)),
            per_tok((PEER_HEADS, 1, tb)),
            pl.BlockSpec((eb, D_MODEL), lambda t, e: (e, 0)),
            pl.BlockSpec((D_MODEL, eb), lambda t, e: (0, e)),
        ],
        out_specs=pl.BlockSpec((D_MODEL, tb), lambda t, e: (0, t)),
        out_shape=jax.ShapeDtypeStruct((D_MODEL, N_PAD), F32),
        scratch_shapes=[pltpu.VMEM((eb, tb), F32), pltpu.VMEM((eb, tb), BF16)],
        compiler_params=_params(("parallel", "arbitrary")),
        name="peer_dense",
    )(xnt, s1t, at, s2t, bnt, tau, u_bf, vt_bf)


FINAL_TM = 256
N_PROMPT_FINAL_BLOCKS = N_PROMPT // FINAL_TM


def _final_kernel(x_ref, pt_ref, p_ref, wg_ref, wp_ref, g_ref, yp_ref, ys_ref):
    i = pl.program_id(0)
    x2 = x_ref[...] + pt_ref[...].T
    gate = _sigmoid(jnp.dot(x2.astype(BF16), wg_ref[...], preferred_element_type=F32))
    emb = jnp.dot(p_ref[...].astype(BF16), wp_ref[...], preferred_element_type=F32)
    y = _rms(x2 + gate * emb, g_ref[...])

    @pl.when(i < N_PROMPT_FINAL_BLOCKS)
    def _():
        yp_ref[...] = y

    @pl.when(i == N_PROMPT_FINAL_BLOCKS)
    def _():
        ys_ref[...] = y[:DEC_BATCH]


def _final(x1, peer_t, p, wg, wp, g):
    tm = FINAL_TM
    assert N_PAD // tm == N_PROMPT_FINAL_BLOCKS + 1
    return pl.pallas_call(
        _final_kernel,
        grid=(N_PAD // tm,),
        in_specs=[
            pl.BlockSpec((tm, D_MODEL), lambda i: (i, 0)),
            pl.BlockSpec((D_MODEL, tm), lambda i: (0, i)),
            pl.BlockSpec((tm, PLE_DIM), lambda i: (i, 0)),
            _const_spec((D_MODEL, D_MODEL)),
            _const_spec((PLE_DIM, D_MODEL)),
            pl.BlockSpec((1, D_MODEL), lambda i: (0, 0)),
        ],
        out_specs=[
            pl.BlockSpec((tm, D_MODEL), lambda i: (jnp.minimum(i, N_PROMPT_FINAL_BLOCKS - 1), 0)),
            pl.BlockSpec((DEC_BATCH, D_MODEL), lambda i: (0, 0)),
        ],
        out_shape=[
            jax.ShapeDtypeStruct((N_PROMPT, D_MODEL), F32),
            jax.ShapeDtypeStruct((DEC_BATCH, D_MODEL), F32),
        ],
        compiler_params=_params(("arbitrary",)),
        name="final",
    )(x1, peer_t, p, wg, wp, g)


KV_ROWS = 512


def _kv_store(k, v, o_ref):
    for h in range(HEADS_PER_GROUP):
        cs = slice(h * HEAD_DIM, (h + 1) * HEAD_DIM)
        o_ref[:, 0, h, :] = k[:, cs]
        o_ref[:, 1, h, :] = v[:, cs]


def _kv_prompt_kernel(k_ref, v_ref, o_ref):
    _kv_store(k_ref[...], v_ref[...], o_ref)


def _kv_prompt(proj, gi, rows):
    blk = min(rows, KV_ROWS)
    nblk = rows // blk
    first = SEQ // blk - nblk
    src = lambda col: pl.BlockSpec(
        (blk, GROUP_WIDTH), lambda b, r: (b * (SEQ // blk) + first + r, col + gi))
    return pl.pallas_call(
        _kv_prompt_kernel,
        grid=(BATCH, nblk),
        in_specs=[src(COL_K), src(COL_V)],
        out_specs=pl.BlockSpec((None, None, blk, 2, HEADS_PER_GROUP, HEAD_DIM),
                               lambda b, r: (0, b, r, 0, 0, 0)),
        out_shape=jax.ShapeDtypeStruct((1, BATCH, rows, 2, HEADS_PER_GROUP, HEAD_DIM), F32),
        compiler_params=_params(("parallel", "parallel")),
        name="kv_prompt",
    )(proj, proj)


def _kv_sample_kernel(k0_ref, v0_ref, k1_ref, v1_ref, k2_ref, v2_ref, o0_ref, o1_ref, o2_ref):
    _kv_store(k0_ref[...], v0_ref[...], o0_ref)
    _kv_store(k1_ref[...], v1_ref[...], o1_ref)
    _kv_store(k2_ref[...], v2_ref[...], o2_ref)


def _kv_sample(proj):
    row_blk = N_PROMPT // DEC_BATCH
    src = lambda col: pl.BlockSpec((DEC_BATCH, GROUP_WIDTH), lambda i: (row_blk, col))
    in_specs = []
    for gi in range(len(DIL_GROUPS)):
        in_specs += [src(COL_K + gi), src(COL_V + gi)]
    out_spec = pl.BlockSpec((None, DEC_BATCH, None, 2, HEADS_PER_GROUP, HEAD_DIM),
                            lambda i: (0, 0, 0, 0, 0, 0))
    out_shape = jax.ShapeDtypeStruct((1, DEC_BATCH, 1, 2, HEADS_PER_GROUP, HEAD_DIM), F32)
    return pl.pallas_call(
        _kv_sample_kernel,
        grid=(1,),
        in_specs=in_specs,
        out_specs=[out_spec] * 3,
        out_shape=[out_shape] * 3,
        compiler_params=_params(("arbitrary",)),
        name="kv_sample",
    )(*([proj] * 6))


def _rope_tables():
    pos = jnp.concatenate([
        jnp.tile(jnp.arange(SEQ, dtype=jnp.int32), BATCH),
        jnp.full((DEC_BATCH,), PAST_LEN, jnp.int32),
        jnp.zeros((N_PAD - N_TOK,), jnp.int32)])
    inv = ROPE_THETA ** (-jnp.arange(ROT_HALF, dtype=F32) / ROT_HALF)
    ang = pos.astype(F32)[:, None] * inv[None, :]
    cos, sin = jnp.cos(ang), jnp.sin(ang)
    n = pos.shape[0]
    zeros = lambda w: jnp.zeros((n, w), F32)
    c = jnp.concatenate([cos, cos, jnp.ones((n, HEAD_DIM - 2 * ROT_HALF), F32)], axis=1)
    s_hi = jnp.concatenate([-sin, zeros(HEAD_DIM - ROT_HALF)], axis=1)
    s_lo = jnp.concatenate([zeros(ROT_HALF), sin, zeros(HEAD_DIM - 2 * ROT_HALF)], axis=1)
    return c, s_hi, s_lo


def kernel(x_prompt, x_sample, cache_kv_w128, cache_kv_w512, cache_kv_w2048, p_prompt, p_sample, g_mix, w_in, sgu_ln_g, sgu_ln_b, w_s, b_s, w_a_out, w_b_out, w_o, g_ffn, peer_w_q, peer_sub_k1, peer_sub_k2, peer_u, peer_v, w_ple, w_ple_gate, g_final):
    assert x_prompt.shape == (BATCH, SEQ, D_MODEL) and x_sample.shape == (DEC_BATCH, 1, D_MODEL)
    assert w_in.shape == (1, D_MODEL, IN_COLS)
    pad = N_PAD - N_TOK
    x = jnp.concatenate([x_prompt.reshape(N_PROMPT, D_MODEL), x_sample.reshape(DEC_BATCH, D_MODEL),
                         jnp.zeros((pad, D_MODEL), F32)], axis=0)
    p = jnp.concatenate([p_prompt.reshape(N_PROMPT, PLE_DIM), p_sample.reshape(DEC_BATCH, PLE_DIM),
                         jnp.zeros((pad, PLE_DIM), F32)], axis=0)

    c, s_hi, s_lo = _rope_tables()
    proj = _proj(x, g_mix, w_in[0].astype(BF16), c, s_hi, s_lo)

    bsb = jnp.broadcast_to(b_s[0][:, :, None], (A_GROUPS, CHUNK, CHUNK))
    wd = jnp.repeat(w_s[0, :, 0, 0], CHUNK)[None]
    b0 = jnp.repeat(b_s[0, :, 0], CHUNK)[None]
    amix, vn_s = _sgu(proj, sgu_ln_g, sgu_ln_b, w_s[0], bsb, wd, b0)

    bmix_p = _attn_prompt(proj)
    cq, ck, cv = COL_Q * PROJ_TN, COL_K * PROJ_TN, COL_V * PROJ_TN
    srows = proj[N_PROMPT:N_TOK]
    per_head = lambda a: a.reshape(DEC_BATCH, len(DIL_GROUPS), HEADS_PER_GROUP, HEAD_DIM)
    caches = []
    for cache, (window, dil) in zip((cache_kv_w128, cache_kv_w512, cache_kv_w2048), DIL_GROUPS):
        l_buf = cache.shape[2]
        assert l_buf == N_KEYS * dil
        caches.append(cache.reshape(DEC_BATCH, N_KEYS, dil, 2, HEADS_PER_GROUP, HEAD_DIM))
    bmix_s = _attn_sample(per_head(srows[:, cq:ck]), per_head(srows[:, ck:cv]), per_head(srows[:, cv:]), caches)
    bmix = jnp.concatenate([bmix_p, bmix_s.reshape(DEC_BATCH, GROUP_WIDTH),
                            jnp.zeros((pad, GROUP_WIDTH), F32)], axis=0)

    x1 = _merge(x, amix, bmix, proj, w_a_out[0].astype(BF16), w_b_out[0].astype(BF16), w_o[0].astype(BF16))

    xnt, s1t, at, s2t, bnt, tau = _peer_gate(x1, g_ffn, peer_w_q[0].astype(BF16),
                                             peer_sub_k1[0].astype(BF16), peer_sub_k2[0].astype(BF16))
    peer_t = _peer_dense(xnt, s1t, at, s2t, bnt, tau, peer_u[0].astype(BF16), peer_v[0].T.astype(BF16))

    y_p, y_s = _final(x1, peer_t, p, w_ple_gate[0].astype(BF16), w_ple[0].astype(BF16), g_final[None])

    y_prompt = y_p.reshape(BATCH, SEQ, D_MODEL)
    y_sample = y_s.reshape(DEC_BATCH, 1, D_MODEL)
    kv_p = [_kv_prompt(proj, gi, min(window, SEQ)) for gi, (window, _) in enumerate(DIL_GROUPS)]
    kv_s = _kv_sample(proj)
    sgu_v_sample = vn_s.reshape(1, DEC_BATCH, 1, A_WIDTH)
    return (y_prompt, y_sample, kv_p[0], kv_p[1], kv_p[2], kv_s[0], kv_s[1], kv_s[2], sgu_v_sample)
```

```python
import functools
import math

import jax
import jax.numpy as jnp
from jax import lax
from jax.experimental import pallas as pl
from jax.experimental.pallas import tpu as pltpu

F32 = jnp.float32
BF16 = jnp.bfloat16

D_MODEL = 2048
BATCH = 4
SEQ = 2048
DEC_BATCH = 128
PAST_LEN = 2048
EPS = 1e-6
CHUNK = 128
A_GROUPS = 8
A_WIDTH = 1024
HEAD_DIM = 128
HEADS_PER_GROUP = 4
DIL_GROUPS = ((128, 1), (512, 4), (2048, 16))
GROUP_WIDTH = HEADS_PER_GROUP * HEAD_DIM
B_WIDTH = 3 * GROUP_WIDTH
ATTN_SCALE = HEAD_DIM ** -0.5
ROPE_THETA = 500000.0
ROT_HALF = HEAD_DIM // 8
IN_COLS = 2 * A_WIDTH + 3 * B_WIDTH + 2 * D_MODEL
N_KEYS = 128
N_EXPERTS = N_KEYS * N_KEYS
PEER_HEADS = 8
PEER_HALF = 128
PEER_TOPK = 16
PLE_DIM = 256

N_PROMPT = BATCH * SEQ
N_TOK = N_PROMPT + DEC_BATCH
N_PAD = 8448
LANE = 128

PROJ_TN = 512
COL_GATE = 2 * A_WIDTH // PROJ_TN
COL_Q = COL_GATE + 2 * D_MODEL // PROJ_TN
COL_K = COL_Q + B_WIDTH // PROJ_TN
COL_V = COL_K + B_WIDTH // PROJ_TN
N_COL_BLOCKS = IN_COLS // PROJ_TN

VMEM_LIMIT = 56 * 1024 * 1024


def _params(sem, vmem=VMEM_LIMIT):
    return pltpu.CompilerParams(dimension_semantics=sem, vmem_limit_bytes=vmem)


def _const_spec(shape):
    nd = len(shape)
    return pl.BlockSpec(shape, lambda *_: (0,) * nd, pipeline_mode=pl.Buffered(1))


def _rms(x, g):
    r = lax.rsqrt(jnp.mean(x * x, axis=-1, keepdims=True) + EPS)
    return (x * r) * g


def _sigmoid(x):
    return 1.0 / (1.0 + jnp.exp(-x))


PROJ_TM = 768
PROJ_CHUNK = 256


PROJ_STEP_COLS = 2 * PROJ_TN
PROJ_COLS = -(-IN_COLS // PROJ_STEP_COLS) * PROJ_STEP_COLS


def _proj_kernel(x_ref, g_ref, wa_ref, wb_ref, c_ref, s1_ref, s2_ref, o_ref, h_ref):
    j = 2 * pl.program_id(1)

    @pl.when(pl.program_id(1) == 0)
    def _():
        h_ref[...] = _rms(x_ref[...], g_ref[...]).astype(BF16)

    def rope(a):
        c = c_ref[...]
        s1 = s1_ref[...]
        s2 = s2_ref[...]
        heads = []
        for hh in range(a.shape[1] // HEAD_DIM):
            ah = a[:, hh * HEAD_DIM:(hh + 1) * HEAD_DIM]
            heads.append(ah * c + pltpu.roll(ah, HEAD_DIM - ROT_HALF, 1) * s1 + pltpu.roll(ah, ROT_HALF, 1) * s2)
        return jnp.concatenate(heads, axis=1)

    def project(epilogue):
        def branch():
            for half, w_ref in enumerate((wa_ref, wb_ref)):
                for c0 in range(0, PROJ_TN, PROJ_CHUNK):
                    acc = jnp.dot(h_ref[...], w_ref[:, c0:c0 + PROJ_CHUNK], preferred_element_type=F32)
                    o0 = half * PROJ_TN + c0
                    o_ref[:, o0:o0 + PROJ_CHUNK] = epilogue(acc)
        return branch

    pl.when(j < COL_GATE)(project(jax.nn.gelu))
    pl.when((j >= COL_GATE) & (j < COL_Q))(project(_sigmoid))
    pl.when((j >= COL_Q) & (j < COL_V))(project(rope))
    pl.when(j >= COL_V)(project(lambda a: a))


def _w_in_col_block(j):
    n_qkv = 3 * B_WIDTH // PROJ_TN
    n_gate = 2 * D_MODEL // PROJ_TN
    j = jnp.minimum(j, N_COL_BLOCKS - 1)
    return jnp.where(j < COL_GATE, j, jnp.where(j < COL_Q, j + n_qkv, j - n_gate))


def _proj(x, g, w, c, s1, s2):
    w_half = lambda half: pl.BlockSpec((D_MODEL, PROJ_TN), lambda i, j: (0, _w_in_col_block(2 * j + half)))
    return pl.pallas_call(
        _proj_kernel,
        grid=(N_PAD // PROJ_TM, PROJ_COLS // PROJ_STEP_COLS),
        in_specs=[
            pl.BlockSpec((PROJ_TM, D_MODEL), lambda i, j: (i, 0)),
            pl.BlockSpec((1, D_MODEL), lambda i, j: (0, 0)),
            w_half(0),
            w_half(1),
            pl.BlockSpec((PROJ_TM, HEAD_DIM), lambda i, j: (i, 0)),
            pl.BlockSpec((PROJ_TM, HEAD_DIM), lambda i, j: (i, 0)),
            pl.BlockSpec((PROJ_TM, HEAD_DIM), lambda i, j: (i, 0)),
        ],
        out_specs=pl.BlockSpec((PROJ_TM, PROJ_STEP_COLS), lambda i, j: (i, j)),
        out_shape=jax.ShapeDtypeStruct((N_PAD, PROJ_COLS), F32),
        scratch_shapes=[pltpu.VMEM((PROJ_TM, D_MODEL), BF16)],
        compiler_params=_params(("parallel", "arbitrary")),
        name="proj",
    )(x, g, w, w, c, s1, s2)


N_PROMPT_CHUNKS = N_PROMPT // CHUNK


def _sgu_kernel(u_ref, gv_ref, lng_ref, lnb_ref, ws_ref, bsb_ref, wd_ref, b0_ref, o_ref, vn_ref):
    i = pl.program_id(0)
    gv = gv_ref[...]
    mu = jnp.mean(gv, axis=-1, keepdims=True)
    var = jnp.mean(jnp.square(gv - mu), axis=-1, keepdims=True)
    vn = ((gv - mu) * lax.rsqrt(var + EPS)) * lng_ref[...] + lnb_ref[...]

    @pl.when(i < N_PROMPT_CHUNKS)
    def _():
        row = lax.broadcasted_iota(jnp.int32, (CHUNK, CHUNK), 0)
        col = lax.broadcasted_iota(jnp.int32, (CHUNK, CHUNK), 1)
        causal = col <= row
        for g in range(A_GROUPS):
            cs = slice(g * CHUNK, (g + 1) * CHUNK)
            w = jnp.where(causal, ws_ref[g], 0.0).astype(BF16)
            mixed = jnp.dot(w, vn[:, cs].astype(BF16), preferred_element_type=F32) + bsb_ref[g]
            o_ref[:, cs] = u_ref[:, cs] * mixed

    @pl.when(i >= N_PROMPT_CHUNKS)
    def _():
        o_ref[...] = u_ref[...] * (vn * wd_ref[...] + b0_ref[...])

    @pl.when(i == N_PROMPT_CHUNKS)
    def _():
        vn_ref[...] = vn


def _sgu(proj, lng, lnb, ws, bsb, wd, b0):
    return pl.pallas_call(
        _sgu_kernel,
        grid=(N_PAD // CHUNK,),
        in_specs=[
            pl.BlockSpec((CHUNK, A_WIDTH), lambda i: (i, 0)),
            pl.BlockSpec((CHUNK, A_WIDTH), lambda i: (i, 1)),
            pl.BlockSpec((1, A_WIDTH), lambda i: (0, 0)),
            pl.BlockSpec((1, A_WIDTH), lambda i: (0, 0)),
            pl.BlockSpec((A_GROUPS, CHUNK, CHUNK), lambda i: (0, 0, 0)),
            pl.BlockSpec((A_GROUPS, CHUNK, CHUNK), lambda i: (0, 0, 0)),
            pl.BlockSpec((1, A_WIDTH), lambda i: (0, 0)),
            pl.BlockSpec((1, A_WIDTH), lambda i: (0, 0)),
        ],
        out_specs=[
            pl.BlockSpec((CHUNK, A_WIDTH), lambda i: (i, 0)),
            pl.BlockSpec((DEC_BATCH, A_WIDTH), lambda i: (0, 0)),
        ],
        out_shape=[
            jax.ShapeDtypeStruct((N_PAD, A_WIDTH), F32),
            jax.ShapeDtypeStruct((DEC_BATCH, A_WIDTH), F32),
        ],
        compiler_params=_params(("arbitrary",)),
        name="sgu",
    )(proj, proj, lng, lnb, ws, bsb, wd, b0)


Q_BLOCK = 128


def _softmax_pv(q, k, v, valid):
    s = lax.dot_general(q, k, (((1,), (1,)), ((), ())), preferred_element_type=F32) * ATTN_SCALE
    s = jnp.where(valid, s, -jnp.inf)
    m = jnp.max(s, axis=-1, keepdims=True)
    e = jnp.exp(s - m)
    den = jnp.sum(e, axis=-1, keepdims=True)
    o = jnp.dot((e / den).astype(BF16), v, preferred_element_type=F32)
    return o, m + jnp.log(den)


STRIDED_GROUPS = tuple(gi for gi, (w, d) in enumerate(DIL_GROUPS) if w >= SEQ and SEQ // d == Q_BLOCK)
BANDED_GROUPS = tuple(gi for gi in range(len(DIL_GROUPS)) if gi not in STRIDED_GROUPS)
assert STRIDED_GROUPS == (2,) and BANDED_GROUPS == (0, 1)


def _attn_strided_kernel(dil, q_ref, k_ref, v_ref, o_ref, l_ref):
    row = lax.broadcasted_iota(jnp.int32, (Q_BLOCK, Q_BLOCK), 0)
    col = lax.broadcasted_iota(jnp.int32, (Q_BLOCK, Q_BLOCK), 1)
    causal = col <= row

    def residue(r, carry):
        rows = pl.ds(r, Q_BLOCK, stride=dil)
        o, lse = _softmax_pv(q_ref[rows, :].astype(BF16), k_ref[rows, :].astype(BF16),
                             v_ref[rows, :].astype(BF16), causal)
        o_ref[rows, :] = o
        l_ref[rows, :] = jnp.broadcast_to(lse, (Q_BLOCK, HEAD_DIM))
        return carry

    lax.fori_loop(0, dil, residue, 0)


def _attn_strided(proj, gi):
    dil = DIL_GROUPS[gi][1]
    heads_per_blk = PROJ_TN // HEAD_DIM
    src = lambda col: pl.BlockSpec(
        (SEQ, HEAD_DIM), lambda b, h: (b, (col + gi) * heads_per_blk + h))
    out = pl.BlockSpec((SEQ, HEAD_DIM), lambda b, h: (b, h))
    shape = jax.ShapeDtypeStruct((N_PROMPT, GROUP_WIDTH), F32)
    return pl.pallas_call(
        functools.partial(_attn_strided_kernel, dil),
        grid=(BATCH, HEADS_PER_GROUP),
        in_specs=[src(COL_Q), src(COL_K), src(COL_V)],
        out_specs=[out, out],
        out_shape=[shape, shape],
        compiler_params=_params(("parallel", "parallel")),
        name="attn_strided",
    )(proj, proj, proj)


def _attn_prompt_kernel(q_ref, k_ref, v_ref, os_ref, ls_ref, o_ref, m_ref):
    g = pl.program_id(1)
    qb = pl.program_id(2)
    t0 = pl.multiple_of(qb * Q_BLOCK, Q_BLOCK)
    rows = pl.ds(t0, Q_BLOCK)

    def group(gi, window, dil):
        n_keys = min(window + Q_BLOCK, SEQ)
        start = pl.multiple_of(jnp.maximum(t0 - window, 0), Q_BLOCK)
        qpos = t0 + lax.broadcasted_iota(jnp.int32, (Q_BLOCK, n_keys), 0)
        kpos = start + lax.broadcasted_iota(jnp.int32, (Q_BLOCK, n_keys), 1)
        dist = qpos - kpos
        okf = jnp.where(dist >= 0, 1.0, 0.0) * jnp.where(dist <= window, 1.0, 0.0)
        okf = okf * jnp.where((dist & (dil - 1)) == 0, 1.0, 0.0)
        valid = okf > 0.5
        for h in range(HEADS_PER_GROUP):
            cs = slice(h * HEAD_DIM, (h + 1) * HEAD_DIM)
            q = q_ref[:, cs].astype(BF16)
            k = k_ref[pl.ds(start, n_keys), cs].astype(BF16)
            v = v_ref[pl.ds(start, n_keys), cs].astype(BF16)
            o, lse = _softmax_pv(q, k, v, valid)
            lse = jnp.broadcast_to(lse, (Q_BLOCK, HEAD_DIM))
            if gi == 0:
                o_ref[rows, cs] = o
                m_ref[rows, cs] = lse
            else:
                o_a, l_a = o_ref[rows, cs], m_ref[rows, cs]
                o_s, l_s = os_ref[:, cs], ls_ref[:, cs]
                top = jnp.maximum(jnp.maximum(l_a, lse), l_s)
                w_a, w_b, w_s = jnp.exp(l_a - top), jnp.exp(lse - top), jnp.exp(l_s - top)
                o_ref[rows, cs] = (o_a * w_a + o * w_b + o_s * w_s) / (w_a + w_b + w_s)

    for gi in BANDED_GROUPS:
        pl.when(g == gi)(functools.partial(group, gi, *DIL_GROUPS[gi]))


def _attn_prompt(proj, o_strided, l_strided):
    n_qb = SEQ // Q_BLOCK
    tile = pl.BlockSpec((Q_BLOCK, GROUP_WIDTH), lambda b, g, qb: (b * n_qb + qb, 0))
    return pl.pallas_call(
        _attn_prompt_kernel,
        grid=(BATCH, len(BANDED_GROUPS), n_qb),
        in_specs=[
            pl.BlockSpec((Q_BLOCK, GROUP_WIDTH), lambda b, g, qb: (b * n_qb + qb, COL_Q + g)),
            pl.BlockSpec((SEQ, GROUP_WIDTH), lambda b, g, qb: (b, COL_K + g)),
            pl.BlockSpec((SEQ, GROUP_WIDTH), lambda b, g, qb: (b, COL_V + g)),
            tile, tile,
        ],
        out_specs=pl.BlockSpec((SEQ, GROUP_WIDTH), lambda b, g, qb: (b, 0)),
        out_shape=jax.ShapeDtypeStruct((N_PROMPT, GROUP_WIDTH), F32),
        scratch_shapes=[pltpu.VMEM((SEQ, GROUP_WIDTH), F32)],
        compiler_params=_params(("parallel", "arbitrary", "arbitrary")),
        name="attn_prompt",
    )(proj, proj, proj, o_strided, l_strided)


SAMPLE_NB = 4


def _attn_sample_kernel(q_ref, kn_ref, vn_ref, k0_ref, v0_ref, k1_ref, v1_ref, k2_ref, v2_ref, o_ref):
    caches = ((k0_ref, v0_ref), (k1_ref, v1_ref), (k2_ref, v2_ref))

    def seq(n, carry):
        outs = []
        lses = []
        for g, (kc_ref, vc_ref) in enumerate(caches):
            q = q_ref[n, g]
            kc = kc_ref[n]
            s = jnp.sum(kc * q[None], axis=-1, keepdims=True) * ATTN_SCALE
            sn = jnp.sum(kn_ref[n, g] * q, axis=-1, keepdims=True) * ATTN_SCALE
            m = jnp.maximum(jnp.max(s, axis=0), sn)
            e = jnp.exp(s - m[None])
            en = jnp.exp(sn - m)
            den = jnp.sum(e, axis=0) + en
            o = (jnp.sum(e * vc_ref[n], axis=0) + en * vn_ref[n, g]) / den
            outs.append(o)
            lses.append(m + jnp.log(den))
        mx = jnp.maximum(jnp.maximum(lses[0], lses[1]), lses[2])
        ws = [jnp.exp(l - mx) for l in lses]
        tot = ws[0] + ws[1] + ws[2]
        o_ref[n] = (ws[0] * outs[0] + ws[1] * outs[1] + ws[2] * outs[2]) / tot
        return carry

    lax.fori_loop(0, SAMPLE_NB, seq, 0)


def _attn_sample(qs, kns, vns, caches):
    nb = SAMPLE_NB
    small = pl.BlockSpec((nb, len(DIL_GROUPS), HEADS_PER_GROUP, HEAD_DIM), lambda n: (n, 0, 0, 0))
    in_specs = [small, small, small]
    args = [qs, kns, vns]
    for c in caches:
        for kv in (0, 1):
            in_specs.append(pl.BlockSpec(
                (nb, N_KEYS, None, None, HEADS_PER_GROUP, HEAD_DIM),
                functools.partial(lambda n, kv_: (n, 0, 0, kv_, 0, 0), kv_=kv)))
            args.append(c)
    return pl.pallas_call(
        _attn_sample_kernel,
        grid=(DEC_BATCH // nb,),
        in_specs=in_specs,
        out_specs=pl.BlockSpec((nb, HEADS_PER_GROUP, HEAD_DIM), lambda n: (n, 0, 0)),
        out_shape=jax.ShapeDtypeStruct((DEC_BATCH, HEADS_PER_GROUP, HEAD_DIM), F32),
        compiler_params=_params(("parallel",)),
        name="attn_sample",
    )(*args)


MERGE_TM = 384


def _merge_kernel(x_ref, a_ref, b_ref, ga_ref, gb_ref, wa_ref, wb_ref, wo_ref, o_ref):
    pa = jnp.dot(a_ref[...].astype(BF16), wa_ref[...], preferred_element_type=F32)
    pb = jnp.dot(b_ref[...].astype(BF16), wb_ref[...], preferred_element_type=F32)
    merged = ga_ref[...] * pa + gb_ref[...] * pb
    o_ref[...] = x_ref[...] + jnp.dot(merged.astype(BF16), wo_ref[...], preferred_element_type=F32)


def _merge(x, amix, bmix, proj, wa, wb, wo):
    tm = MERGE_TM
    gate_a_blk = COL_GATE * PROJ_TN // D_MODEL
    return pl.pallas_call(
        _merge_kernel,
        grid=(N_PAD // tm,),
        in_specs=[
            pl.BlockSpec((tm, D_MODEL), lambda i: (i, 0)),
            pl.BlockSpec((tm, A_WIDTH), lambda i: (i, 0)),
            pl.BlockSpec((tm, GROUP_WIDTH), lambda i: (i, 0)),
            pl.BlockSpec((tm, D_MODEL), lambda i: (i, gate_a_blk)),
            pl.BlockSpec((tm, D_MODEL), lambda i: (i, gate_a_blk + 1)),
            _const_spec((A_WIDTH, D_MODEL)),
            _const_spec((GROUP_WIDTH, D_MODEL)),
            _const_spec((D_MODEL, D_MODEL)),
        ],
        out_specs=pl.BlockSpec((tm, D_MODEL), lambda i: (i, 0)),
        out_shape=jax.ShapeDtypeStruct((N_PAD, D_MODEL), F32),
        compiler_params=_params(("parallel",)),
        name="merge",
    )(x, amix, bmix, proj, proj, wa, wb, wo)


GATE_TB = 128
PAIRS_PER_RANK = tuple(PEER_TOPK // (a + 1) for a in range(PEER_TOPK))


SUBLANES = 8


def _batcher_pairs(n):
    pairs = []

    def merge(lo, m, r):
        step = 2 * r
        if step < m:
            merge(lo, m, step)
            merge(lo + r, m, step)
            pairs.extend((i, i + r) for i in range(lo + r, lo + m - r, step))
        else:
            pairs.append((lo, lo + r))

    def sort(lo, m):
        if m > 1:
            sort(lo, m // 2)
            sort(lo + m // 2, m // 2)
            merge(lo, m, 1)

    sort(0, n)
    return tuple(pairs)


SORT16_PAIRS = _batcher_pairs(PEER_TOPK)


def _exchange(v, i, j):
    v[i], v[j] = jnp.maximum(v[i], v[j]), jnp.minimum(v[i], v[j])


def _bitonic_sort_desc(c):
    c = list(c)
    d = PEER_TOPK // 2
    while d:
        for k in range(PEER_TOPK):
            if not k & d:
                _exchange(c, k, k + d)
        d //= 2
    return c


def _merge_sublanes(v, sort_last):
    for shift in (4, 2, 1):
        other = [pltpu.roll(x, shift, 0) for x in v]
        c = [jnp.maximum(v[k], other[PEER_TOPK - 1 - k]) for k in range(PEER_TOPK)]
        v = c if (shift == 1 and not sort_last) else _bitonic_sort_desc(c)
    return v


def _top16_sorted(s):
    v = [s[k * SUBLANES:(k + 1) * SUBLANES] for k in range(s.shape[0] // SUBLANES)]
    for i, j in SORT16_PAIRS:
        _exchange(v, i, j)
    return _merge_sublanes(v, True)


def _by_sublane(rows, sub8):
    out = rows[SUBLANES - 1]
    for b in range(SUBLANES - 2, -1, -1):
        out = jnp.where(sub8 == b, rows[b], out)
    return out


def _peer_gate_kernel(x_ref, g_ref, wq_ref, k1_ref, k2_ref,
                      xnt_ref, s1_ref, a_ref, s2_ref, bn_ref, tau_ref):
    xn = _rms(x_ref[...], g_ref[...])
    xnt_ref[...] = xn.T.astype(BF16)
    qp = jnp.dot(xn.astype(BF16), wq_ref[...], preferred_element_type=F32).astype(BF16)
    sub8 = lax.broadcasted_iota(jnp.int32, (8, GATE_TB), 0)
    nt = (((1,), (1,)), ((), ()))
    for h in range(PEER_HEADS):
        c0 = h * 2 * PEER_HALF
        s1 = lax.dot_general(k1_ref[...], qp[:, c0:c0 + PEER_HALF], nt, preferred_element_type=F32)
        s2 = lax.dot_general(k2_ref[...], qp[:, c0 + PEER_HALF:c0 + 2 * PEER_HALF], nt,
                             preferred_element_type=F32)
        t1 = _top16_sorted(s1)
        t2 = _top16_sorted(s2)
        t2_lo = _by_sublane(t2[:SUBLANES], sub8)
        t2_hi = _by_sublane(t2[SUBLANES:], sub8)
        lo = []
        for a in range(PEER_TOPK):
            nvalid = min(SUBLANES, PAIRS_PER_RANK[a])
            c = t1[a] + t2_lo
            lo.append(c if nvalid == SUBLANES else jnp.where(sub8 < nvalid, c, -jnp.inf))
        hi = t1[0] + t2_hi
        ins = [jnp.maximum(lo[0], hi)]
        ins += [jnp.maximum(lo[k], jnp.minimum(lo[k - 1], hi)) for k in range(1, PEER_TOPK)]
        best = _merge_sublanes(ins, False)
        tau = functools.reduce(jnp.minimum, best)
        e2_lo = jnp.exp(t2_lo - t2[0])
        zacc = jnp.where(hi >= tau, jnp.exp(t2_hi - t2[0]), 0.0)
        for a in range(PEER_TOPK):
            zacc = zacc + jnp.where(lo[a] >= tau, jnp.exp(t1[a] - t1[0]) * e2_lo, 0.0)
        z = jnp.sum(zacc, axis=0, keepdims=True)
        s1_ref[h] = s1
        s2_ref[h] = s2
        a_ref[h] = jnp.exp(s1 - t1[0][0:1])
        bn_ref[h] = jnp.exp(s2 - t2[0][0:1]) / z
        tau_ref[h] = tau[0:1]


def _peer_gate(x1, g, wq, k1, k2):
    tb = GATE_TB
    big = pl.BlockSpec((PEER_HEADS, N_KEYS, tb), lambda i: (0, 0, i))
    big_shape = jax.ShapeDtypeStruct((PEER_HEADS, N_KEYS, N_PAD), F32)
    return pl.pallas_call(
        _peer_gate_kernel,
        grid=(N_PAD // tb,),
        in_specs=[
            pl.BlockSpec((tb, D_MODEL), lambda i: (i, 0)),
            pl.BlockSpec((1, D_MODEL), lambda i: (0, 0)),
            _const_spec((D_MODEL, PEER_HEADS * 2 * PEER_HALF)),
            _const_spec((N_KEYS, PEER_HALF)),
            _const_spec((N_KEYS, PEER_HALF)),
        ],
        out_specs=[
            pl.BlockSpec((D_MODEL, tb), lambda i: (0, i)),
            big, big, big, big,
            pl.BlockSpec((PEER_HEADS, 1, tb), lambda i: (0, 0, i)),
        ],
        out_shape=[
            jax.ShapeDtypeStruct((D_MODEL, N_PAD), BF16),
            big_shape, big_shape, big_shape, big_shape,
            jax.ShapeDtypeStruct((PEER_HEADS, 1, N_PAD), F32),
        ],
        compiler_params=_params(("parallel",)),
        name="peer_gate",
    )(x1, g, wq, k1, k2)


DENSE_TB = 768
DENSE_EB = 1024
DENSE_NI = DENSE_EB // N_KEYS


def _peer_dense_kernel(xnt_ref, s1_ref, a_ref, s2_ref, bn_ref, tau_ref, u_ref, vt_ref,
                       o_ref, st_ref, wt_ref):
    e = pl.program_id(1)

    @pl.when(e == 0)
    def _():
        o_ref[...] = jnp.zeros_like(o_ref)

    st_ref[...] = jnp.dot(u_ref[...], xnt_ref[...], preferred_element_type=F32)

    for ii in range(DENSE_NI):
        rows = slice(ii * N_KEYS, (ii + 1) * N_KEYS)
        for c in range(DENSE_TB // LANE):
            lanes = slice(c * LANE, (c + 1) * LANE)
            gate = jnp.zeros((N_KEYS, LANE), F32)
            for h in range(PEER_HEADS):
                s1_row = s1_ref[h, ii:ii + 1, lanes]
                a_row = a_ref[h, ii:ii + 1, lanes]
                hit = (s1_row + s2_ref[h, :, lanes]) >= tau_ref[h, :, lanes]
                gate = gate + jnp.where(hit, a_row * bn_ref[h, :, lanes], 0.0)
            act = jax.nn.gelu(st_ref[rows, lanes])
            wt_ref[rows, lanes] = (gate * act).astype(BF16)

    o_ref[...] += jnp.dot(vt_ref[...], wt_ref[...], preferred_element_type=F32)


def _peer_dense(xnt, s1t, at, s2t, bnt, tau, u_bf, vt_bf):
    tb, eb = DENSE_TB, DENSE_EB
    per_tok = lambda shape: pl.BlockSpec(shape, lambda t, e: (0,) * (len(shape) - 1) + (t,),
                                         pipeline_mode=pl.Buffered(1))
    by_key = pl.BlockSpec((PEER_HEADS, DENSE_NI, tb), lambda t, e: (0, e, t))
    return pl.pallas_call(
        _peer_dense_kernel,
        grid=(N_PAD // tb, N_EXPERTS // eb),
        in_specs=[
            per_tok((D_MODEL, tb)),
            by_key, by_key,
            per_tok((PEER_HEADS, N_KEYS, tb)),
            per_tok((PEER_HEADS, N_KEYS, tb)),
            per_tok((PEER_HEADS, 1, tb)),
            pl.BlockSpec((eb, D_MODEL), lambda t, e: (e, 0)),
            pl.BlockSpec((None, D_MODEL, eb), lambda t, e: (e, 0, 0)),
        ],
        out_specs=pl.BlockSpec((D_MODEL, tb), lambda t, e: (0, t)),
        out_shape=jax.ShapeDtypeStruct((D_MODEL, N_PAD), F32),
        scratch_shapes=[pltpu.VMEM((eb, tb), F32), pltpu.VMEM((eb, tb), BF16)],
        compiler_params=_params(("parallel", "arbitrary")),
        name="peer_dense",
    )(xnt, s1t, at, s2t, bnt, tau, u_bf, vt_bf)


FINAL_TM = 256
N_PROMPT_FINAL_BLOCKS = N_PROMPT // FINAL_TM


def _final_kernel(x_ref, pt_ref, p_ref, wg_ref, wp_ref, g_ref, yp_ref, ys_ref):
    i = pl.program_id(0)
    x2 = x_ref[...] + pt_ref[...].T
    gate = _sigmoid(jnp.dot(x2.astype(BF16), wg_ref[...], preferred_element_type=F32))
    emb = jnp.dot(p_ref[...].astype(BF16), wp_ref[...], preferred_element_type=F32)
    y = _rms(x2 + gate * emb, g_ref[...])

    @pl.when(i < N_PROMPT_FINAL_BLOCKS)
    def _():
        yp_ref[...] = y

    @pl.when(i == N_PROMPT_FINAL_BLOCKS)
    def _():
        ys_ref[...] = y[:DEC_BATCH]


def _final(x1, peer_t, p, wg, wp, g):
    tm = FINAL_TM
    assert N_PAD // tm == N_PROMPT_FINAL_BLOCKS + 1
    return pl.pallas_call(
        _final_kernel,
        grid=(N_PAD // tm,),
        in_specs=[
            pl.BlockSpec((tm, D_MODEL), lambda i: (i, 0)),
            pl.BlockSpec((D_MODEL, tm), lambda i: (0, i)),
            pl.BlockSpec((tm, PLE_DIM), lambda i: (i, 0)),
            _const_spec((D_MODEL, D_MODEL)),
            _const_spec((PLE_DIM, D_MODEL)),
            pl.BlockSpec((1, D_MODEL), lambda i: (0, 0)),
        ],
        out_specs=[
            pl.BlockSpec((tm, D_MODEL), lambda i: (jnp.minimum(i, N_PROMPT_FINAL_BLOCKS - 1), 0)),
            pl.BlockSpec((DEC_BATCH, D_MODEL), lambda i: (0, 0)),
        ],
        out_shape=[
            jax.ShapeDtypeStruct((N_PROMPT, D_MODEL), F32),
            jax.ShapeDtypeStruct((DEC_BATCH, D_MODEL), F32),
        ],
        compiler_params=_params(("arbitrary",)),
        name="final",
    )(x1, peer_t, p, wg, wp, g)


KV_ROWS = 512


def _kv_store(k, v, o_ref):
    for h in range(HEADS_PER_GROUP):
        cs = slice(h * HEAD_DIM, (h + 1) * HEAD_DIM)
        o_ref[:, 0, h, :] = k[:, cs]
        o_ref[:, 1, h, :] = v[:, cs]


def _kv_prompt_kernel(k_ref, v_ref, o_ref):
    _kv_store(k_ref[...], v_ref[...], o_ref)


def _kv_prompt(proj, gi, rows):
    blk = min(rows, KV_ROWS)
    nblk = rows // blk
    first = SEQ // blk - nblk
    src = lambda col: pl.BlockSpec(
        (blk, GROUP_WIDTH), lambda b, r: (b * (SEQ // blk) + first + r, col + gi))
    return pl.pallas_call(
        _kv_prompt_kernel,
        grid=(BATCH, nblk),
        in_specs=[src(COL_K), src(COL_V)],
        out_specs=pl.BlockSpec((None, None, blk, 2, HEADS_PER_GROUP, HEAD_DIM),
                               lambda b, r: (0, b, r, 0, 0, 0)),
        out_shape=jax.ShapeDtypeStruct((1, BATCH, rows, 2, HEADS_PER_GROUP, HEAD_DIM), F32),
        compiler_params=_params(("parallel", "parallel")),
        name="kv_prompt",
    )(proj, proj)


def _kv_sample_kernel(k0_ref, v0_ref, k1_ref, v1_ref, k2_ref, v2_ref, o0_ref, o1_ref, o2_ref):
    _kv_store(k0_ref[...], v0_ref[...], o0_ref)
    _kv_store(k1_ref[...], v1_ref[...], o1_ref)
    _kv_store(k2_ref[...], v2_ref[...], o2_ref)


def _kv_sample(proj):
    row_blk = N_PROMPT // DEC_BATCH
    src = lambda col: pl.BlockSpec((DEC_BATCH, GROUP_WIDTH), lambda i: (row_blk, col))
    in_specs = []
    for gi in range(len(DIL_GROUPS)):
        in_specs += [src(COL_K + gi), src(COL_V + gi)]
    out_spec = pl.BlockSpec((None, DEC_BATCH, None, 2, HEADS_PER_GROUP, HEAD_DIM),
                            lambda i: (0, 0, 0, 0, 0, 0))
    out_shape = jax.ShapeDtypeStruct((1, DEC_BATCH, 1, 2, HEADS_PER_GROUP, HEAD_DIM), F32)
    return pl.pallas_call(
        _kv_sample_kernel,
        grid=(1,),
        in_specs=in_specs,
        out_specs=[out_spec] * 3,
        out_shape=[out_shape] * 3,
        compiler_params=_params(("arbitrary",)),
        name="kv_sample",
    )(*([proj] * 6))


def _rope_tables():
    pos = jnp.concatenate([
        jnp.tile(jnp.arange(SEQ, dtype=jnp.int32), BATCH),
        jnp.full((DEC_BATCH,), PAST_LEN, jnp.int32),
        jnp.zeros((N_PAD - N_TOK,), jnp.int32)])
    inv = ROPE_THETA ** (-jnp.arange(ROT_HALF, dtype=F32) / ROT_HALF)
    ang = pos.astype(F32)[:, None] * inv[None, :]
    cos, sin = jnp.cos(ang), jnp.sin(ang)
    n = pos.shape[0]
    zeros = lambda w: jnp.zeros((n, w), F32)
    c = jnp.concatenate([cos, cos, jnp.ones((n, HEAD_DIM - 2 * ROT_HALF), F32)], axis=1)
    s_hi = jnp.concatenate([-sin, zeros(HEAD_DIM - ROT_HALF)], axis=1)
    s_lo = jnp.concatenate([zeros(ROT_HALF), sin, zeros(HEAD_DIM - 2 * ROT_HALF)], axis=1)
    return c, s_hi, s_lo


def kernel(x_prompt, x_sample, cache_kv_w128, cache_kv_w512, cache_kv_w2048, p_prompt, p_sample, g_mix, w_in, sgu_ln_g, sgu_ln_b, w_s, b_s, w_a_out, w_b_out, w_o, g_ffn, peer_w_q, peer_sub_k1, peer_sub_k2, peer_u, peer_v, w_ple, w_ple_gate, g_final):
    assert x_prompt.shape == (BATCH, SEQ, D_MODEL) and x_sample.shape == (DEC_BATCH, 1, D_MODEL)
    assert w_in.shape == (1, D_MODEL, IN_COLS)
    pad = N_PAD - N_TOK
    x = jnp.concatenate([x_prompt.reshape(N_PROMPT, D_MODEL), x_sample.reshape(DEC_BATCH, D_MODEL),
                         jnp.zeros((pad, D_MODEL), F32)], axis=0)
    p = jnp.concatenate([p_prompt.reshape(N_PROMPT, PLE_DIM), p_sample.reshape(DEC_BATCH, PLE_DIM),
                         jnp.zeros((pad, PLE_DIM), F32)], axis=0)

    c, s_hi, s_lo = _rope_tables()
    proj = _proj(x, g_mix, w_in[0].astype(BF16), c, s_hi, s_lo)

    bsb = jnp.broadcast_to(b_s[0][:, :, None], (A_GROUPS, CHUNK, CHUNK))
    wd = jnp.repeat(w_s[0, :, 0, 0], CHUNK)[None]
    b0 = jnp.repeat(b_s[0, :, 0], CHUNK)[None]
    amix, vn_s = _sgu(proj, sgu_ln_g, sgu_ln_b, w_s[0], bsb, wd, b0)

    bmix_p = _attn_prompt(proj, *_attn_strided(proj, STRIDED_GROUPS[0]))
    cq, ck, cv = COL_Q * PROJ_TN, COL_K * PROJ_TN, COL_V * PROJ_TN
    srows = proj[N_PROMPT:N_TOK]
    per_head = lambda a: a.reshape(DEC_BATCH, len(DIL_GROUPS), HEADS_PER_GROUP, HEAD_DIM)
    caches = []
    for cache, (window, dil) in zip((cache_kv_w128, cache_kv_w512, cache_kv_w2048), DIL_GROUPS):
        l_buf = cache.shape[2]
        assert l_buf == N_KEYS * dil
        caches.append(cache.reshape(DEC_BATCH, N_KEYS, dil, 2, HEADS_PER_GROUP, HEAD_DIM))
    bmix_s = _attn_sample(per_head(srows[:, cq:ck]), per_head(srows[:, ck:cv]), per_head(srows[:, cv:cv + B_WIDTH]), caches)
    bmix = jnp.concatenate([bmix_p, bmix_s.reshape(DEC_BATCH, GROUP_WIDTH),
                            jnp.zeros((pad, GROUP_WIDTH), F32)], axis=0)

    x1 = _merge(x, amix, bmix, proj, w_a_out[0].astype(BF16), w_b_out[0].astype(BF16), w_o[0].astype(BF16))

    xnt, s1t, at, s2t, bnt, tau = _peer_gate(x1, g_ffn, peer_w_q[0].astype(BF16),
                                             peer_sub_k1[0].astype(BF16), peer_sub_k2[0].astype(BF16))
    vt = peer_v[0].reshape(N_EXPERTS // DENSE_EB, DENSE_EB, D_MODEL).transpose(0, 2, 1).astype(BF16)
    peer_t = _peer_dense(xnt, s1t, at, s2t, bnt, tau, peer_u[0].astype(BF16), vt)

    y_p, y_s = _final(x1, peer_t, p, w_ple_gate[0].astype(BF16), w_ple[0].astype(BF16), g_final[None])

    y_prompt = y_p.reshape(BATCH, SEQ, D_MODEL)
    y_sample = y_s.reshape(DEC_BATCH, 1, D_MODEL)
    kv_p = [_kv_prompt(proj, gi, min(window, SEQ)) for gi, (window, _) in enumerate(DIL_GROUPS)]
    kv_s = _kv_sample(proj)
    sgu_v_sample = vn_s.reshape(1, DEC_BATCH, 1, A_WIDTH)
    return (y_prompt, y_sample, kv_p[0], kv_p[1], kv_p[2], kv_s[0], kv_s[1], kv_s[2], sgu_v_sample)
```

```python
import functools
import math

import jax
import jax.numpy as jnp
from jax import lax
from jax.experimental import pallas as pl
from jax.experimental.pallas import tpu as pltpu

F32 = jnp.float32
BF16 = jnp.bfloat16

D_MODEL = 2048
BATCH = 4
SEQ = 2048
DEC_BATCH = 128
PAST_LEN = 2048
EPS = 1e-6
CHUNK = 128
A_GROUPS = 8
A_WIDTH = 1024
HEAD_DIM = 128
HEADS_PER_GROUP = 4
DIL_GROUPS = ((128, 1), (512, 4), (2048, 16))
GROUP_WIDTH = HEADS_PER_GROUP * HEAD_DIM
B_WIDTH = 3 * GROUP_WIDTH
ATTN_SCALE = HEAD_DIM ** -0.5
ROPE_THETA = 500000.0
ROT_HALF = HEAD_DIM // 8
IN_COLS = 2 * A_WIDTH + 3 * B_WIDTH + 2 * D_MODEL
N_KEYS = 128
N_EXPERTS = N_KEYS * N_KEYS
PEER_HEADS = 8
PEER_HALF = 128
PEER_TOPK = 16
PLE_DIM = 256

N_PROMPT = BATCH * SEQ
N_TOK = N_PROMPT + DEC_BATCH
N_PAD = 8448
LANE = 128

PROJ_TN = 512
COL_GATE = 2 * A_WIDTH // PROJ_TN
COL_Q = COL_GATE + 2 * D_MODEL // PROJ_TN
COL_K = COL_Q + B_WIDTH // PROJ_TN
COL_V = COL_K + B_WIDTH // PROJ_TN
N_COL_BLOCKS = IN_COLS // PROJ_TN

VMEM_LIMIT = 56 * 1024 * 1024


def _params(sem, vmem=VMEM_LIMIT):
    return pltpu.CompilerParams(dimension_semantics=sem, vmem_limit_bytes=vmem)


def _const_spec(shape):
    nd = len(shape)
    return pl.BlockSpec(shape, lambda *_: (0,) * nd, pipeline_mode=pl.Buffered(1))


def _rms(x, g):
    r = lax.rsqrt(jnp.mean(x * x, axis=-1, keepdims=True) + EPS)
    return (x * r) * g


def _sigmoid(x):
    return 1.0 / (1.0 + jnp.exp(-x))


PROJ_TM = 768
PROJ_CHUNK = 256


PROJ_STEP_COLS = 2 * PROJ_TN
PROJ_COLS = -(-IN_COLS // PROJ_STEP_COLS) * PROJ_STEP_COLS


def _proj_kernel(x_ref, g_ref, wa_ref, wb_ref, c_ref, s1_ref, s2_ref, o_ref, h_ref):
    j = 2 * pl.program_id(1)

    @pl.when(pl.program_id(1) == 0)
    def _():
        h_ref[...] = _rms(x_ref[...], g_ref[...]).astype(BF16)

    def rope(a):
        c = c_ref[...]
        s1 = s1_ref[...]
        s2 = s2_ref[...]
        heads = []
        for hh in range(a.shape[1] // HEAD_DIM):
            ah = a[:, hh * HEAD_DIM:(hh + 1) * HEAD_DIM]
            heads.append(ah * c + pltpu.roll(ah, HEAD_DIM - ROT_HALF, 1) * s1 + pltpu.roll(ah, ROT_HALF, 1) * s2)
        return jnp.concatenate(heads, axis=1)

    def project(epilogue):
        def branch():
            for half, w_ref in enumerate((wa_ref, wb_ref)):
                for c0 in range(0, PROJ_TN, PROJ_CHUNK):
                    acc = jnp.dot(h_ref[...], w_ref[:, c0:c0 + PROJ_CHUNK], preferred_element_type=F32)
                    o0 = half * PROJ_TN + c0
                    o_ref[:, o0:o0 + PROJ_CHUNK] = epilogue(acc)
        return branch

    pl.when(j < COL_GATE)(project(jax.nn.gelu))
    pl.when((j >= COL_GATE) & (j < COL_Q))(project(_sigmoid))
    pl.when((j >= COL_Q) & (j < COL_V))(project(rope))
    pl.when(j >= COL_V)(project(lambda a: a))


def _w_in_col_block(j):
    n_qkv = 3 * B_WIDTH // PROJ_TN
    n_gate = 2 * D_MODEL // PROJ_TN
    j = jnp.minimum(j, N_COL_BLOCKS - 1)
    return jnp.where(j < COL_GATE, j, jnp.where(j < COL_Q, j + n_qkv, j - n_gate))


def _proj(x, g, w, c, s1, s2):
    w_half = lambda half: pl.BlockSpec((D_MODEL, PROJ_TN), lambda i, j: (0, _w_in_col_block(2 * j + half)))
    return pl.pallas_call(
        _proj_kernel,
        grid=(N_PAD // PROJ_TM, PROJ_COLS // PROJ_STEP_COLS),
        in_specs=[
            pl.BlockSpec((PROJ_TM, D_MODEL), lambda i, j: (i, 0)),
            pl.BlockSpec((1, D_MODEL), lambda i, j: (0, 0)),
            w_half(0),
            w_half(1),
            pl.BlockSpec((PROJ_TM, HEAD_DIM), lambda i, j: (i, 0)),
            pl.BlockSpec((PROJ_TM, HEAD_DIM), lambda i, j: (i, 0)),
            pl.BlockSpec((PROJ_TM, HEAD_DIM), lambda i, j: (i, 0)),
        ],
        out_specs=pl.BlockSpec((PROJ_TM, PROJ_STEP_COLS), lambda i, j: (i, j)),
        out_shape=jax.ShapeDtypeStruct((N_PAD, PROJ_COLS), F32),
        scratch_shapes=[pltpu.VMEM((PROJ_TM, D_MODEL), BF16)],
        compiler_params=_params(("parallel", "arbitrary")),
        name="proj",
    )(x, g, w, w, c, s1, s2)


N_PROMPT_CHUNKS = N_PROMPT // CHUNK


def _sgu_kernel(u_ref, gv_ref, lng_ref, lnb_ref, ws_ref, bsb_ref, wd_ref, b0_ref, o_ref, vn_ref):
    i = pl.program_id(0)
    gv = gv_ref[...]
    mu = jnp.mean(gv, axis=-1, keepdims=True)
    var = jnp.mean(jnp.square(gv - mu), axis=-1, keepdims=True)
    vn = ((gv - mu) * lax.rsqrt(var + EPS)) * lng_ref[...] + lnb_ref[...]

    @pl.when(i < N_PROMPT_CHUNKS)
    def _():
        row = lax.broadcasted_iota(jnp.int32, (CHUNK, CHUNK), 0)
        col = lax.broadcasted_iota(jnp.int32, (CHUNK, CHUNK), 1)
        causal = col <= row
        for g in range(A_GROUPS):
            cs = slice(g * CHUNK, (g + 1) * CHUNK)
            w = jnp.where(causal, ws_ref[g], 0.0).astype(BF16)
            mixed = jnp.dot(w, vn[:, cs].astype(BF16), preferred_element_type=F32) + bsb_ref[g]
            o_ref[:, cs] = u_ref[:, cs] * mixed

    @pl.when(i >= N_PROMPT_CHUNKS)
    def _():
        o_ref[...] = u_ref[...] * (vn * wd_ref[...] + b0_ref[...])

    @pl.when(i == N_PROMPT_CHUNKS)
    def _():
        vn_ref[...] = vn


def _sgu(proj, lng, lnb, ws, bsb, wd, b0):
    return pl.pallas_call(
        _sgu_kernel,
        grid=(N_PAD // CHUNK,),
        in_specs=[
            pl.BlockSpec((CHUNK, A_WIDTH), lambda i: (i, 0)),
            pl.BlockSpec((CHUNK, A_WIDTH), lambda i: (i, 1)),
            pl.BlockSpec((1, A_WIDTH), lambda i: (0, 0)),
            pl.BlockSpec((1, A_WIDTH), lambda i: (0, 0)),
            pl.BlockSpec((A_GROUPS, CHUNK, CHUNK), lambda i: (0, 0, 0)),
            pl.BlockSpec((A_GROUPS, CHUNK, CHUNK), lambda i: (0, 0, 0)),
            pl.BlockSpec((1, A_WIDTH), lambda i: (0, 0)),
            pl.BlockSpec((1, A_WIDTH), lambda i: (0, 0)),
        ],
        out_specs=[
            pl.BlockSpec((CHUNK, A_WIDTH), lambda i: (i, 0)),
            pl.BlockSpec((DEC_BATCH, A_WIDTH), lambda i: (0, 0)),
        ],
        out_shape=[
            jax.ShapeDtypeStruct((N_PAD, A_WIDTH), F32),
            jax.ShapeDtypeStruct((DEC_BATCH, A_WIDTH), F32),
        ],
        compiler_params=_params(("arbitrary",)),
        name="sgu",
    )(proj, proj, lng, lnb, ws, bsb, wd, b0)


Q_BLOCK = 128


def _softmax_pv(q, k, v, valid):
    s = lax.dot_general(q, k, (((1,), (1,)), ((), ())), preferred_element_type=F32) * ATTN_SCALE
    s = jnp.where(valid, s, -jnp.inf)
    m = jnp.max(s, axis=-1, keepdims=True)
    e = jnp.exp(s - m)
    den = jnp.sum(e, axis=-1, keepdims=True)
    o = jnp.dot((e / den).astype(BF16), v, preferred_element_type=F32)
    return o, m + jnp.log(den)


STRIDED_GROUPS = tuple(gi for gi, (_, d) in enumerate(DIL_GROUPS) if d > 1)
assert STRIDED_GROUPS == (1, 2) and all(w // d == Q_BLOCK for w, d in DIL_GROUPS)
STRIDED_UNROLL = 4


def _attn_strided_kernel(dil, q_ref, k_ref, v_ref, o_ref, l_ref):
    n_blk = SEQ // dil // Q_BLOCK
    n_keys = Q_BLOCK * min(n_blk, 2)
    dist0 = (lax.broadcasted_iota(jnp.int32, (Q_BLOCK, n_keys), 0)
             - lax.broadcasted_iota(jnp.int32, (Q_BLOCK, n_keys), 1))

    def tiles(it, carry):
        for u in range(STRIDED_UNROLL):
            tile = it * STRIDED_UNROLL + u
            r, n = tile // n_blk, tile % n_blk
            if n_blk == STRIDED_UNROLL:
                r, n = it, u
            first_key_blk = jnp.maximum(n - 1, 0)
            q_rows = pl.ds(r + dil * Q_BLOCK * n, Q_BLOCK, stride=dil)
            k_rows = pl.ds(r + dil * Q_BLOCK * first_key_blk, n_keys, stride=dil)
            dist = dist0 + Q_BLOCK * (n - first_key_blk)
            valid = jnp.where(dist >= 0, 1.0, 0.0) * jnp.where(dist <= Q_BLOCK, 1.0, 0.0) > 0.5
            o, lse = _softmax_pv(q_ref[q_rows, :].astype(BF16), k_ref[k_rows, :].astype(BF16),
                                 v_ref[k_rows, :].astype(BF16), valid)
            o_ref[q_rows, :] = o
            l_ref[q_rows, :] = jnp.broadcast_to(lse, (Q_BLOCK, HEAD_DIM))
        return carry

    lax.fori_loop(0, dil * n_blk // STRIDED_UNROLL, tiles, 0)


def _attn_strided(proj, gi):
    dil = DIL_GROUPS[gi][1]
    assert (SEQ // dil) % Q_BLOCK == 0 and (dil * (SEQ // dil // Q_BLOCK)) % STRIDED_UNROLL == 0
    heads_per_blk = PROJ_TN // HEAD_DIM
    src = lambda col: pl.BlockSpec(
        (SEQ, HEAD_DIM), lambda b, h: (b, (col + gi) * heads_per_blk + h))
    out = pl.BlockSpec((SEQ, HEAD_DIM), lambda b, h: (b, h))
    shape = jax.ShapeDtypeStruct((N_PROMPT, GROUP_WIDTH), F32)
    return pl.pallas_call(
        functools.partial(_attn_strided_kernel, dil),
        grid=(BATCH, HEADS_PER_GROUP),
        in_specs=[src(COL_Q), src(COL_K), src(COL_V)],
        out_specs=[out, out],
        out_shape=[shape, shape],
        compiler_params=_params(("parallel", "parallel")),
        name="attn_strided",
    )(proj, proj, proj)


def _attn_prompt_kernel(q_ref, k_ref, v_ref, o1_ref, l1_ref, o2_ref, l2_ref, o_ref):
    window = DIL_GROUPS[0][0]
    t0 = pl.multiple_of(pl.program_id(1) * Q_BLOCK, Q_BLOCK)
    n_keys = window + Q_BLOCK
    start = pl.multiple_of(jnp.maximum(t0 - window, 0), Q_BLOCK)
    dist = ((t0 - start) + lax.broadcasted_iota(jnp.int32, (Q_BLOCK, n_keys), 0)
            - lax.broadcasted_iota(jnp.int32, (Q_BLOCK, n_keys), 1))
    valid = jnp.where(dist >= 0, 1.0, 0.0) * jnp.where(dist <= window, 1.0, 0.0) > 0.5
    for h in range(HEADS_PER_GROUP):
        cs = slice(h * HEAD_DIM, (h + 1) * HEAD_DIM)
        o0, l0 = _softmax_pv(q_ref[:, cs].astype(BF16), k_ref[pl.ds(start, n_keys), cs].astype(BF16),
                             v_ref[pl.ds(start, n_keys), cs].astype(BF16), valid)
        l1, l2 = l1_ref[:, cs], l2_ref[:, cs]
        top = jnp.maximum(jnp.maximum(l0, l1), l2)
        w0, w1, w2 = jnp.exp(l0 - top), jnp.exp(l1 - top), jnp.exp(l2 - top)
        o_ref[:, cs] = (o0 * w0 + o1_ref[:, cs] * w1 + o2_ref[:, cs] * w2) / (w0 + w1 + w2)


def _attn_prompt(proj, strided):
    n_qb = SEQ // Q_BLOCK
    tile = pl.BlockSpec((Q_BLOCK, GROUP_WIDTH), lambda b, qb: (b * n_qb + qb, 0))
    (o1, l1), (o2, l2) = strided
    return pl.pallas_call(
        _attn_prompt_kernel,
        grid=(BATCH, n_qb),
        in_specs=[
            pl.BlockSpec((Q_BLOCK, GROUP_WIDTH), lambda b, qb: (b * n_qb + qb, COL_Q)),
            pl.BlockSpec((SEQ, GROUP_WIDTH), lambda b, qb: (b, COL_K)),
            pl.BlockSpec((SEQ, GROUP_WIDTH), lambda b, qb: (b, COL_V)),
            tile, tile, tile, tile,
        ],
        out_specs=tile,
        out_shape=jax.ShapeDtypeStruct((N_PROMPT, GROUP_WIDTH), F32),
        compiler_params=_params(("parallel", "arbitrary")),
        name="attn_prompt",
    )(proj, proj, proj, o1, l1, o2, l2)


SAMPLE_NB = 4


def _attn_sample_kernel(q_ref, kn_ref, vn_ref, k0_ref, v0_ref, k1_ref, v1_ref, k2_ref, v2_ref, o_ref):
    caches = ((k0_ref, v0_ref), (k1_ref, v1_ref), (k2_ref, v2_ref))

    def seq(n, carry):
        outs = []
        lses = []
        for g, (kc_ref, vc_ref) in enumerate(caches):
            q = q_ref[n, g]
            kc = kc_ref[n]
            s = jnp.sum(kc * q[None], axis=-1, keepdims=True) * ATTN_SCALE
            sn = jnp.sum(kn_ref[n, g] * q, axis=-1, keepdims=True) * ATTN_SCALE
            m = jnp.maximum(jnp.max(s, axis=0), sn)
            e = jnp.exp(s - m[None])
            en = jnp.exp(sn - m)
            den = jnp.sum(e, axis=0) + en
            o = (jnp.sum(e * vc_ref[n], axis=0) + en * vn_ref[n, g]) / den
            outs.append(o)
            lses.append(m + jnp.log(den))
        mx = jnp.maximum(jnp.maximum(lses[0], lses[1]), lses[2])
        ws = [jnp.exp(l - mx) for l in lses]
        tot = ws[0] + ws[1] + ws[2]
        o_ref[n] = (ws[0] * outs[0] + ws[1] * outs[1] + ws[2] * outs[2]) / tot
        return carry

    lax.fori_loop(0, SAMPLE_NB, seq, 0)


def _attn_sample(qs, kns, vns, caches):
    nb = SAMPLE_NB
    small = pl.BlockSpec((nb, len(DIL_GROUPS), HEADS_PER_GROUP, HEAD_DIM), lambda n: (n, 0, 0, 0))
    in_specs = [small, small, small]
    args = [qs, kns, vns]
    for c in caches:
        for kv in (0, 1):
            in_specs.append(pl.BlockSpec(
                (nb, N_KEYS, None, None, HEADS_PER_GROUP, HEAD_DIM),
                functools.partial(lambda n, kv_: (n, 0, 0, kv_, 0, 0), kv_=kv)))
            args.append(c)
    return pl.pallas_call(
        _attn_sample_kernel,
        grid=(DEC_BATCH // nb,),
        in_specs=in_specs,
        out_specs=pl.BlockSpec((nb, HEADS_PER_GROUP, HEAD_DIM), lambda n: (n, 0, 0)),
        out_shape=jax.ShapeDtypeStruct((DEC_BATCH, HEADS_PER_GROUP, HEAD_DIM), F32),
        compiler_params=_params(("parallel",)),
        name="attn_sample",
    )(*args)


MERGE_TM = 384


def _merge_kernel(x_ref, a_ref, b_ref, ga_ref, gb_ref, wa_ref, wb_ref, wo_ref, o_ref):
    pa = jnp.dot(a_ref[...].astype(BF16), wa_ref[...], preferred_element_type=F32)
    pb = jnp.dot(b_ref[...].astype(BF16), wb_ref[...], preferred_element_type=F32)
    merged = ga_ref[...] * pa + gb_ref[...] * pb
    o_ref[...] = x_ref[...] + jnp.dot(merged.astype(BF16), wo_ref[...], preferred_element_type=F32)


def _merge(x, amix, bmix, proj, wa, wb, wo):
    tm = MERGE_TM
    gate_a_blk = COL_GATE * PROJ_TN // D_MODEL
    return pl.pallas_call(
        _merge_kernel,
        grid=(N_PAD // tm,),
        in_specs=[
            pl.BlockSpec((tm, D_MODEL), lambda i: (i, 0)),
            pl.BlockSpec((tm, A_WIDTH), lambda i: (i, 0)),
            pl.BlockSpec((tm, GROUP_WIDTH), lambda i: (i, 0)),
            pl.BlockSpec((tm, D_MODEL), lambda i: (i, gate_a_blk)),
            pl.BlockSpec((tm, D_MODEL), lambda i: (i, gate_a_blk + 1)),
            _const_spec((A_WIDTH, D_MODEL)),
            _const_spec((GROUP_WIDTH, D_MODEL)),
            _const_spec((D_MODEL, D_MODEL)),
        ],
        out_specs=pl.BlockSpec((tm, D_MODEL), lambda i: (i, 0)),
        out_shape=jax.ShapeDtypeStruct((N_PAD, D_MODEL), F32),
        compiler_params=_params(("parallel",)),
        name="merge",
    )(x, amix, bmix, proj, proj, wa, wb, wo)


GATE_TB = 128
PAIRS_PER_RANK = tuple(PEER_TOPK // (a + 1) for a in range(PEER_TOPK))


SUBLANES = 8


def _batcher_pairs(n):
    pairs = []

    def merge(lo, m, r):
        step = 2 * r
        if step < m:
            merge(lo, m, step)
            merge(lo + r, m, step)
            pairs.extend((i, i + r) for i in range(lo + r, lo + m - r, step))
        else:
            pairs.append((lo, lo + r))

    def sort(lo, m):
        if m > 1:
            sort(lo, m // 2)
            sort(lo + m // 2, m // 2)
            merge(lo, m, 1)

    sort(0, n)
    return tuple(pairs)


SORT16_PAIRS = _batcher_pairs(PEER_TOPK)


def _exchange(v, i, j):
    v[i], v[j] = jnp.maximum(v[i], v[j]), jnp.minimum(v[i], v[j])


def _bitonic_sort_desc(c):
    c = list(c)
    d = PEER_TOPK // 2
    while d:
        for k in range(PEER_TOPK):
            if not k & d:
                _exchange(c, k, k + d)
        d //= 2
    return c


def _merge_sublanes(v, sort_last):
    for shift in (4, 2, 1):
        other = [pltpu.roll(x, shift, 0) for x in v]
        c = [jnp.maximum(v[k], other[PEER_TOPK - 1 - k]) for k in range(PEER_TOPK)]
        v = c if (shift == 1 and not sort_last) else _bitonic_sort_desc(c)
    return v


def _top16_sorted(s):
    v = [s[k * SUBLANES:(k + 1) * SUBLANES] for k in range(s.shape[0] // SUBLANES)]
    for i, j in SORT16_PAIRS:
        _exchange(v, i, j)
    return _merge_sublanes(v, True)


def _by_sublane(rows, sub8):
    out = rows[SUBLANES - 1]
    for b in range(SUBLANES - 2, -1, -1):
        out = jnp.where(sub8 == b, rows[b], out)
    return out


def _peer_gate_kernel(x_ref, g_ref, wq_ref, k1_ref, k2_ref,
                      xnt_ref, s1_ref, a_ref, s2_ref, bn_ref, tau_ref):
    xn = _rms(x_ref[...], g_ref[...])
    xnt_ref[...] = xn.T.astype(BF16)
    qp = jnp.dot(xn.astype(BF16), wq_ref[...], preferred_element_type=F32).astype(BF16)
    sub8 = lax.broadcasted_iota(jnp.int32, (8, GATE_TB), 0)
    nt = (((1,), (1,)), ((), ()))
    for h in range(PEER_HEADS):
        c0 = h * 2 * PEER_HALF
        s1 = lax.dot_general(k1_ref[...], qp[:, c0:c0 + PEER_HALF], nt, preferred_element_type=F32)
        s2 = lax.dot_general(k2_ref[...], qp[:, c0 + PEER_HALF:c0 + 2 * PEER_HALF], nt,
                             preferred_element_type=F32)
        t1 = _top16_sorted(s1)
        t2 = _top16_sorted(s2)
        t2_lo = _by_sublane(t2[:SUBLANES], sub8)
        t2_hi = _by_sublane(t2[SUBLANES:], sub8)
        lo = []
        for a in range(PEER_TOPK):
            nvalid = min(SUBLANES, PAIRS_PER_RANK[a])
            c = t1[a] + t2_lo
            lo.append(c if nvalid == SUBLANES else jnp.where(sub8 < nvalid, c, -jnp.inf))
        hi = t1[0] + t2_hi
        ins = [jnp.maximum(lo[0], hi)]
        ins += [jnp.maximum(lo[k], jnp.minimum(lo[k - 1], hi)) for k in range(1, PEER_TOPK)]
        best = _merge_sublanes(ins, False)
        tau = functools.reduce(jnp.minimum, best)
        e2_lo = jnp.exp(t2_lo - t2[0])
        zacc = jnp.where(hi >= tau, jnp.exp(t2_hi - t2[0]), 0.0)
        for a in range(PEER_TOPK):
            zacc = zacc + jnp.where(lo[a] >= tau, jnp.exp(t1[a] - t1[0]) * e2_lo, 0.0)
        z = jnp.sum(zacc, axis=0, keepdims=True)
        s1_ref[h] = s1
        s2_ref[h] = s2
        a_ref[h] = jnp.exp(s1 - t1[0][0:1])
        bn_ref[h] = jnp.exp(s2 - t2[0][0:1]) / z
        tau_ref[h] = tau[0:1]


def _peer_gate(x1, g, wq, k1, k2):
    tb = GATE_TB
    big = pl.BlockSpec((PEER_HEADS, N_KEYS, tb), lambda i: (0, 0, i))
    big_shape = jax.ShapeDtypeStruct((PEER_HEADS, N_KEYS, N_PAD), F32)
    return pl.pallas_call(
        _peer_gate_kernel,
        grid=(N_PAD // tb,),
        in_specs=[
            pl.BlockSpec((tb, D_MODEL), lambda i: (i, 0)),
            pl.BlockSpec((1, D_MODEL), lambda i: (0, 0)),
            _const_spec((D_MODEL, PEER_HEADS * 2 * PEER_HALF)),
            _const_spec((N_KEYS, PEER_HALF)),
            _const_spec((N_KEYS, PEER_HALF)),
        ],
        out_specs=[
            pl.BlockSpec((D_MODEL, tb), lambda i: (0, i)),
            big, big, big, big,
            pl.BlockSpec((PEER_HEADS, 1, tb), lambda i: (0, 0, i)),
        ],
        out_shape=[
            jax.ShapeDtypeStruct((D_MODEL, N_PAD), BF16),
            big_shape, big_shape, big_shape, big_shape,
            jax.ShapeDtypeStruct((PEER_HEADS, 1, N_PAD), F32),
        ],
        compiler_params=_params(("parallel",)),
        name="peer_gate",
    )(x1, g, wq, k1, k2)


DENSE_TB = 768
DENSE_EB = 1024
DENSE_NI = DENSE_EB // N_KEYS


def _peer_dense_kernel(xnt_ref, s1_ref, a_ref, s2_ref, bn_ref, tau_ref, u_ref, vt_ref,
                       o_ref, st_ref, wt_ref):
    e = pl.program_id(1)

    @pl.when(e == 0)
    def _():
        o_ref[...] = jnp.zeros_like(o_ref)

    st_ref[...] = jnp.dot(u_ref[...], xnt_ref[...], preferred_element_type=F32)

    for ii in range(DENSE_NI):
        rows = slice(ii * N_KEYS, (ii + 1) * N_KEYS)
        for c in range(DENSE_TB // LANE):
            lanes = slice(c * LANE, (c + 1) * LANE)
            gate = jnp.zeros((N_KEYS, LANE), F32)
            for h in range(PEER_HEADS):
                s1_row = s1_ref[h, ii:ii + 1, lanes]
                a_row = a_ref[h, ii:ii + 1, lanes]
                hit = (s1_row + s2_ref[h, :, lanes]) >= tau_ref[h, :, lanes]
                gate = gate + jnp.where(hit, a_row * bn_ref[h, :, lanes], 0.0)
            act = jax.nn.gelu(st_ref[rows, lanes])
            wt_ref[rows, lanes] = (gate * act).astype(BF16)

    o_ref[...] += jnp.dot(vt_ref[...], wt_ref[...], preferred_element_type=F32)


def _peer_dense(xnt, s1t, at, s2t, bnt, tau, u_bf, vt_bf):
    tb, eb = DENSE_TB, DENSE_EB
    per_tok = lambda shape: pl.BlockSpec(shape, lambda t, e: (0,) * (len(shape) - 1) + (t,),
                                         pipeline_mode=pl.Buffered(1))
    by_key = pl.BlockSpec((PEER_HEADS, DENSE_NI, tb), lambda t, e: (0, e, t))
    return pl.pallas_call(
        _peer_dense_kernel,
        grid=(N_PAD // tb, N_EXPERTS // eb),
        in_specs=[
            per_tok((D_MODEL, tb)),
            by_key, by_key,
            per_tok((PEER_HEADS, N_KEYS, tb)),
            per_tok((PEER_HEADS, N_KEYS, tb)),
            per_tok((PEER_HEADS, 1, tb)),
            pl.BlockSpec((eb, D_MODEL), lambda t, e: (e, 0)),
            pl.BlockSpec((None, D_MODEL, eb), lambda t, e: (e, 0, 0)),
        ],
        out_specs=pl.BlockSpec((D_MODEL, tb), lambda t, e: (0, t)),
        out_shape=jax.ShapeDtypeStruct((D_MODEL, N_PAD), F32),
        scratch_shapes=[pltpu.VMEM((eb, tb), F32), pltpu.VMEM((eb, tb), BF16)],
        compiler_params=_params(("parallel", "arbitrary")),
        name="peer_dense",
    )(xnt, s1t, at, s2t, bnt, tau, u_bf, vt_bf)


FINAL_TM = 256
N_PROMPT_FINAL_BLOCKS = N_PROMPT // FINAL_TM


def _final_kernel(x_ref, pt_ref, p_ref, wg_ref, wp_ref, g_ref, yp_ref, ys_ref):
    i = pl.program_id(0)
    x2 = x_ref[...] + pt_ref[...].T
    gate = _sigmoid(jnp.dot(x2.astype(BF16), wg_ref[...], preferred_element_type=F32))
    emb = jnp.dot(p_ref[...].astype(BF16), wp_ref[...], preferred_element_type=F32)
    y = _rms(x2 + gate * emb, g_ref[...])

    @pl.when(i < N_PROMPT_FINAL_BLOCKS)
    def _():
        yp_ref[...] = y

    @pl.when(i == N_PROMPT_FINAL_BLOCKS)
    def _():
        ys_ref[...] = y[:DEC_BATCH]


def _final(x1, peer_t, p, wg, wp, g):
    tm = FINAL_TM
    assert N_PAD // tm == N_PROMPT_FINAL_BLOCKS + 1
    return pl.pallas_call(
        _final_kernel,
        grid=(N_PAD // tm,),
        in_specs=[
            pl.BlockSpec((tm, D_MODEL), lambda i: (i, 0)),
            pl.BlockSpec((D_MODEL, tm), lambda i: (0, i)),
            pl.BlockSpec((tm, PLE_DIM), lambda i: (i, 0)),
            _const_spec((D_MODEL, D_MODEL)),
            _const_spec((PLE_DIM, D_MODEL)),
            pl.BlockSpec((1, D_MODEL), lambda i: (0, 0)),
        ],
        out_specs=[
            pl.BlockSpec((tm, D_MODEL), lambda i: (jnp.minimum(i, N_PROMPT_FINAL_BLOCKS - 1), 0)),
            pl.BlockSpec((DEC_BATCH, D_MODEL), lambda i: (0, 0)),
        ],
        out_shape=[
            jax.ShapeDtypeStruct((N_PROMPT, D_MODEL), F32),
            jax.ShapeDtypeStruct((DEC_BATCH, D_MODEL), F32),
        ],
        compiler_params=_params(("arbitrary",)),
        name="final",
    )(x1, peer_t, p, wg, wp, g)


KV_ROWS = 512


def _kv_store(k, v, o_ref):
    for h in range(HEADS_PER_GROUP):
        cs = slice(h * HEAD_DIM, (h + 1) * HEAD_DIM)
        o_ref[:, 0, h, :] = k[:, cs]
        o_ref[:, 1, h, :] = v[:, cs]


def _kv_prompt_kernel(k_ref, v_ref, o_ref):
    _kv_store(k_ref[...], v_ref[...], o_ref)


def _kv_prompt(proj, gi, rows):
    blk = min(rows, KV_ROWS)
    nblk = rows // blk
    first = SEQ // blk - nblk
    src = lambda col: pl.BlockSpec(
        (blk, GROUP_WIDTH), lambda b, r: (b * (SEQ // blk) + first + r, col + gi))
    return pl.pallas_call(
        _kv_prompt_kernel,
        grid=(BATCH, nblk),
        in_specs=[src(COL_K), src(COL_V)],
        out_specs=pl.BlockSpec((None, None, blk, 2, HEADS_PER_GROUP, HEAD_DIM),
                               lambda b, r: (0, b, r, 0, 0, 0)),
        out_shape=jax.ShapeDtypeStruct((1, BATCH, rows, 2, HEADS_PER_GROUP, HEAD_DIM), F32),
        compiler_params=_params(("parallel", "parallel")),
        name="kv_prompt",
    )(proj, proj)


def _kv_sample_kernel(k0_ref, v0_ref, k1_ref, v1_ref, k2_ref, v2_ref, o0_ref, o1_ref, o2_ref):
    _kv_store(k0_ref[...], v0_ref[...], o0_ref)
    _kv_store(k1_ref[...], v1_ref[...], o1_ref)
    _kv_store(k2_ref[...], v2_ref[...], o2_ref)


def _kv_sample(proj):
    row_blk = N_PROMPT // DEC_BATCH
    src = lambda col: pl.BlockSpec((DEC_BATCH, GROUP_WIDTH), lambda i: (row_blk, col))
    in_specs = []
    for gi in range(len(DIL_GROUPS)):
        in_specs += [src(COL_K + gi), src(COL_V + gi)]
    out_spec = pl.BlockSpec((None, DEC_BATCH, None, 2, HEADS_PER_GROUP, HEAD_DIM),
                            lambda i: (0, 0, 0, 0, 0, 0))
    out_shape = jax.ShapeDtypeStruct((1, DEC_BATCH, 1, 2, HEADS_PER_GROUP, HEAD_DIM), F32)
    return pl.pallas_call(
        _kv_sample_kernel,
        grid=(1,),
        in_specs=in_specs,
        out_specs=[out_spec] * 3,
        out_shape=[out_shape] * 3,
        compiler_params=_params(("arbitrary",)),
        name="kv_sample",
    )(*([proj] * 6))


def _rope_tables():
    pos = jnp.concatenate([
        jnp.tile(jnp.arange(SEQ, dtype=jnp.int32), BATCH),
        jnp.full((DEC_BATCH,), PAST_LEN, jnp.int32),
        jnp.zeros((N_PAD - N_TOK,), jnp.int32)])
    inv = ROPE_THETA ** (-jnp.arange(ROT_HALF, dtype=F32) / ROT_HALF)
    ang = pos.astype(F32)[:, None] * inv[None, :]
    cos, sin = jnp.cos(ang), jnp.sin(ang)
    n = pos.shape[0]
    zeros = lambda w: jnp.zeros((n, w), F32)
    c = jnp.concatenate([cos, cos, jnp.ones((n, HEAD_DIM - 2 * ROT_HALF), F32)], axis=1)
    s_hi = jnp.concatenate([-sin, zeros(HEAD_DIM - ROT_HALF)], axis=1)
    s_lo = jnp.concatenate([zeros(ROT_HALF), sin, zeros(HEAD_DIM - 2 * ROT_HALF)], axis=1)
    return c, s_hi, s_lo


def kernel(x_prompt, x_sample, cache_kv_w128, cache_kv_w512, cache_kv_w2048, p_prompt, p_sample, g_mix, w_in, sgu_ln_g, sgu_ln_b, w_s, b_s, w_a_out, w_b_out, w_o, g_ffn, peer_w_q, peer_sub_k1, peer_sub_k2, peer_u, peer_v, w_ple, w_ple_gate, g_final):
    assert x_prompt.shape == (BATCH, SEQ, D_MODEL) and x_sample.shape == (DEC_BATCH, 1, D_MODEL)
    assert w_in.shape == (1, D_MODEL, IN_COLS)
    pad = N_PAD - N_TOK
    x = jnp.concatenate([x_prompt.reshape(N_PROMPT, D_MODEL), x_sample.reshape(DEC_BATCH, D_MODEL),
                         jnp.zeros((pad, D_MODEL), F32)], axis=0)
    p = jnp.concatenate([p_prompt.reshape(N_PROMPT, PLE_DIM), p_sample.reshape(DEC_BATCH, PLE_DIM),
                         jnp.zeros((pad, PLE_DIM), F32)], axis=0)

    c, s_hi, s_lo = _rope_tables()
    proj = _proj(x, g_mix, w_in[0].astype(BF16), c, s_hi, s_lo)

    bsb = jnp.broadcast_to(b_s[0][:, :, None], (A_GROUPS, CHUNK, CHUNK))
    wd = jnp.repeat(w_s[0, :, 0, 0], CHUNK)[None]
    b0 = jnp.repeat(b_s[0, :, 0], CHUNK)[None]
    amix, vn_s = _sgu(proj, sgu_ln_g, sgu_ln_b, w_s[0], bsb, wd, b0)

    bmix_p = _attn_prompt(proj, [_attn_strided(proj, gi) for gi in STRIDED_GROUPS])
    cq, ck, cv = COL_Q * PROJ_TN, COL_K * PROJ_TN, COL_V * PROJ_TN
    srows = proj[N_PROMPT:N_TOK]
    per_head = lambda a: a.reshape(DEC_BATCH, len(DIL_GROUPS), HEADS_PER_GROUP, HEAD_DIM)
    caches = []
    for cache, (window, dil) in zip((cache_kv_w128, cache_kv_w512, cache_kv_w2048), DIL_GROUPS):
        l_buf = cache.shape[2]
        assert l_buf == N_KEYS * dil
        caches.append(cache.reshape(DEC_BATCH, N_KEYS, dil, 2, HEADS_PER_GROUP, HEAD_DIM))
    bmix_s = _attn_sample(per_head(srows[:, cq:ck]), per_head(srows[:, ck:cv]), per_head(srows[:, cv:cv + B_WIDTH]), caches)
    bmix = jnp.concatenate([bmix_p, bmix_s.reshape(DEC_BATCH, GROUP_WIDTH),
                            jnp.zeros((pad, GROUP_WIDTH), F32)], axis=0)

    x1 = _merge(x, amix, bmix, proj, w_a_out[0].astype(BF16), w_b_out[0].astype(BF16), w_o[0].astype(BF16))

    xnt, s1t, at, s2t, bnt, tau = _peer_gate(x1, g_ffn, peer_w_q[0].astype(BF16),
                                             peer_sub_k1[0].astype(BF16), peer_sub_k2[0].astype(BF16))
    vt = peer_v[0].reshape(N_EXPERTS // DENSE_EB, DENSE_EB, D_MODEL).transpose(0, 2, 1).astype(BF16)
    peer_t = _peer_dense(xnt, s1t, at, s2t, bnt, tau, peer_u[0].astype(BF16), vt)

    y_p, y_s = _final(x1, peer_t, p, w_ple_gate[0].astype(BF16), w_ple[0].astype(BF16), g_final[None])

    y_prompt = y_p.reshape(BATCH, SEQ, D_MODEL)
    y_sample = y_s.reshape(DEC_BATCH, 1, D_MODEL)
    kv_p = [_kv_prompt(proj, gi, min(window, SEQ)) for gi, (window, _) in enumerate(DIL_GROUPS)]
    kv_s = _kv_sample(proj)
    sgu_v_sample = vn_s.reshape(1, DEC_BATCH, 1, A_WIDTH)
    return (y_prompt, y_sample, kv_p[0], kv_p[1], kv_p[2], kv_s[0], kv_s[1], kv_s[2], sgu_v_sample)
```

```python
import functools
import math

import jax
import jax.numpy as jnp
from jax import lax
from jax.experimental import pallas as pl
from jax.experimental.pallas import tpu as pltpu

F32 = jnp.float32
BF16 = jnp.bfloat16

D_MODEL = 2048
BATCH = 4
SEQ = 2048
DEC_BATCH = 128
PAST_LEN = 2048
EPS = 1e-6
CHUNK = 128
A_GROUPS = 8
A_WIDTH = 1024
HEAD_DIM = 128
HEADS_PER_GROUP = 4
DIL_GROUPS = ((128, 1), (512, 4), (2048, 16))
GROUP_WIDTH = HEADS_PER_GROUP * HEAD_DIM
B_WIDTH = 3 * GROUP_WIDTH
ATTN_SCALE = HEAD_DIM ** -0.5
ROPE_THETA = 500000.0
ROT_HALF = HEAD_DIM // 8
IN_COLS = 2 * A_WIDTH + 3 * B_WIDTH + 2 * D_MODEL
N_KEYS = 128
N_EXPERTS = N_KEYS * N_KEYS
PEER_HEADS = 8
PEER_HALF = 128
PEER_TOPK = 16
PLE_DIM = 256

N_PROMPT = BATCH * SEQ
N_TOK = N_PROMPT + DEC_BATCH
N_PAD = 8448
LANE = 128

PROJ_TN = 512
COL_GATE = 2 * A_WIDTH // PROJ_TN
COL_Q = COL_GATE + 2 * D_MODEL // PROJ_TN
COL_K = COL_Q + B_WIDTH // PROJ_TN
COL_V = COL_K + B_WIDTH // PROJ_TN
N_COL_BLOCKS = IN_COLS // PROJ_TN

VMEM_LIMIT = 56 * 1024 * 1024


def _params(sem, vmem=VMEM_LIMIT):
    return pltpu.CompilerParams(dimension_semantics=sem, vmem_limit_bytes=vmem)


def _const_spec(shape):
    nd = len(shape)
    return pl.BlockSpec(shape, lambda *_: (0,) * nd, pipeline_mode=pl.Buffered(1))


def _rms(x, g):
    r = lax.rsqrt(jnp.mean(x * x, axis=-1, keepdims=True) + EPS)
    return (x * r) * g


def _sigmoid(x):
    return 1.0 / (1.0 + jnp.exp(-x))


PROJ_TM = 768
PROJ_CHUNK = 256


PROJ_STEP_COLS = 2 * PROJ_TN
PROJ_COLS = -(-IN_COLS // PROJ_STEP_COLS) * PROJ_STEP_COLS


def _proj_kernel(x_ref, g_ref, wa_ref, wb_ref, c_ref, s1_ref, s2_ref, o_ref, h_ref):
    j = 2 * pl.program_id(1)

    @pl.when(pl.program_id(1) == 0)
    def _():
        h_ref[...] = _rms(x_ref[...], g_ref[...]).astype(BF16)

    def rope(a):
        c = c_ref[...]
        s1 = s1_ref[...]
        s2 = s2_ref[...]
        heads = []
        for hh in range(a.shape[1] // HEAD_DIM):
            ah = a[:, hh * HEAD_DIM:(hh + 1) * HEAD_DIM]
            heads.append(ah * c + pltpu.roll(ah, HEAD_DIM - ROT_HALF, 1) * s1 + pltpu.roll(ah, ROT_HALF, 1) * s2)
        return jnp.concatenate(heads, axis=1)

    def project(epilogue):
        def branch():
            for half, w_ref in enumerate((wa_ref, wb_ref)):
                for c0 in range(0, PROJ_TN, PROJ_CHUNK):
                    acc = jnp.dot(h_ref[...], w_ref[:, c0:c0 + PROJ_CHUNK], preferred_element_type=F32)
                    o0 = half * PROJ_TN + c0
                    o_ref[:, o0:o0 + PROJ_CHUNK] = epilogue(acc)
        return branch

    pl.when(j < COL_GATE)(project(jax.nn.gelu))
    pl.when((j >= COL_GATE) & (j < COL_Q))(project(_sigmoid))
    pl.when((j >= COL_Q) & (j < COL_V))(project(rope))
    pl.when(j >= COL_V)(project(lambda a: a))


def _w_in_col_block(j):
    n_qkv = 3 * B_WIDTH // PROJ_TN
    n_gate = 2 * D_MODEL // PROJ_TN
    j = jnp.minimum(j, N_COL_BLOCKS - 1)
    return jnp.where(j < COL_GATE, j, jnp.where(j < COL_Q, j + n_qkv, j - n_gate))


def _proj(x, g, w, c, s1, s2):
    w_half = lambda half: pl.BlockSpec((D_MODEL, PROJ_TN), lambda i, j: (0, _w_in_col_block(2 * j + half)))
    return pl.pallas_call(
        _proj_kernel,
        grid=(N_PAD // PROJ_TM, PROJ_COLS // PROJ_STEP_COLS),
        in_specs=[
            pl.BlockSpec((PROJ_TM, D_MODEL), lambda i, j: (i, 0)),
            pl.BlockSpec((1, D_MODEL), lambda i, j: (0, 0)),
            w_half(0),
            w_half(1),
            pl.BlockSpec((PROJ_TM, HEAD_DIM), lambda i, j: (i, 0)),
            pl.BlockSpec((PROJ_TM, HEAD_DIM), lambda i, j: (i, 0)),
            pl.BlockSpec((PROJ_TM, HEAD_DIM), lambda i, j: (i, 0)),
        ],
        out_specs=pl.BlockSpec((PROJ_TM, PROJ_STEP_COLS), lambda i, j: (i, j)),
        out_shape=jax.ShapeDtypeStruct((N_PAD, PROJ_COLS), F32),
        scratch_shapes=[pltpu.VMEM((PROJ_TM, D_MODEL), BF16)],
        compiler_params=_params(("parallel", "arbitrary")),
        name="proj",
    )(x, g, w, w, c, s1, s2)


N_PROMPT_CHUNKS = N_PROMPT // CHUNK


def _sgu_kernel(u_ref, gv_ref, lng_ref, lnb_ref, ws_ref, bsb_ref, wd_ref, b0_ref, o_ref, vn_ref):
    i = pl.program_id(0)
    gv = gv_ref[...]
    mu = jnp.mean(gv, axis=-1, keepdims=True)
    var = jnp.mean(jnp.square(gv - mu), axis=-1, keepdims=True)
    vn = ((gv - mu) * lax.rsqrt(var + EPS)) * lng_ref[...] + lnb_ref[...]

    @pl.when(i < N_PROMPT_CHUNKS)
    def _():
        row = lax.broadcasted_iota(jnp.int32, (CHUNK, CHUNK), 0)
        col = lax.broadcasted_iota(jnp.int32, (CHUNK, CHUNK), 1)
        causal = col <= row
        for g in range(A_GROUPS):
            cs = slice(g * CHUNK, (g + 1) * CHUNK)
            w = jnp.where(causal, ws_ref[g], 0.0).astype(BF16)
            mixed = jnp.dot(w, vn[:, cs].astype(BF16), preferred_element_type=F32) + bsb_ref[g]
            o_ref[:, cs] = u_ref[:, cs] * mixed

    @pl.when(i >= N_PROMPT_CHUNKS)
    def _():
        o_ref[...] = u_ref[...] * (vn * wd_ref[...] + b0_ref[...])

    @pl.when(i == N_PROMPT_CHUNKS)
    def _():
        vn_ref[...] = vn


def _sgu(proj, lng, lnb, ws, bsb, wd, b0):
    return pl.pallas_call(
        _sgu_kernel,
        grid=(N_PAD // CHUNK,),
        in_specs=[
            pl.BlockSpec((CHUNK, A_WIDTH), lambda i: (i, 0)),
            pl.BlockSpec((CHUNK, A_WIDTH), lambda i: (i, 1)),
            pl.BlockSpec((1, A_WIDTH), lambda i: (0, 0)),
            pl.BlockSpec((1, A_WIDTH), lambda i: (0, 0)),
            pl.BlockSpec((A_GROUPS, CHUNK, CHUNK), lambda i: (0, 0, 0)),
            pl.BlockSpec((A_GROUPS, CHUNK, CHUNK), lambda i: (0, 0, 0)),
            pl.BlockSpec((1, A_WIDTH), lambda i: (0, 0)),
            pl.BlockSpec((1, A_WIDTH), lambda i: (0, 0)),
        ],
        out_specs=[
            pl.BlockSpec((CHUNK, A_WIDTH), lambda i: (i, 0)),
            pl.BlockSpec((DEC_BATCH, A_WIDTH), lambda i: (0, 0)),
        ],
        out_shape=[
            jax.ShapeDtypeStruct((N_PAD, A_WIDTH), F32),
            jax.ShapeDtypeStruct((DEC_BATCH, A_WIDTH), F32),
        ],
        compiler_params=_params(("arbitrary",)),
        name="sgu",
    )(proj, proj, lng, lnb, ws, bsb, wd, b0)


Q_BLOCK = 128


def _softmax_pv(q, k, v, valid):
    s = lax.dot_general(q, k, (((1,), (1,)), ((), ())), preferred_element_type=F32) * ATTN_SCALE
    s = jnp.where(valid, s, -jnp.inf)
    m = jnp.max(s, axis=-1, keepdims=True)
    e = jnp.exp(s - m)
    den = jnp.sum(e, axis=-1, keepdims=True)
    o = jnp.dot((e / den).astype(BF16), v, preferred_element_type=F32)
    return o, m + jnp.log(den)


STRIDED_GROUPS = tuple(gi for gi, (_, d) in enumerate(DIL_GROUPS) if d > 1)
assert STRIDED_GROUPS == (1, 2) and all(w // d == Q_BLOCK for w, d in DIL_GROUPS)
STRIDED_UNROLL = 4


def _attn_strided_kernel(dil, q_ref, k_ref, v_ref, o_ref, l_ref):
    n_blk = SEQ // dil // Q_BLOCK
    n_keys = Q_BLOCK * min(n_blk, 2)
    dist0 = (lax.broadcasted_iota(jnp.int32, (Q_BLOCK, n_keys), 0)
             - lax.broadcasted_iota(jnp.int32, (Q_BLOCK, n_keys), 1))

    def tiles(it, carry):
        for u in range(STRIDED_UNROLL):
            tile = it * STRIDED_UNROLL + u
            r, n = tile // n_blk, tile % n_blk
            if n_blk == STRIDED_UNROLL:
                r, n = it, u
            first_key_blk = jnp.maximum(n - 1, 0)
            q_rows = pl.ds(r + dil * Q_BLOCK * n, Q_BLOCK, stride=dil)
            k_rows = pl.ds(r + dil * Q_BLOCK * first_key_blk, n_keys, stride=dil)
            dist = dist0 + Q_BLOCK * (n - first_key_blk)
            valid = jnp.where(dist >= 0, 1.0, 0.0) * jnp.where(dist <= Q_BLOCK, 1.0, 0.0) > 0.5
            o, lse = _softmax_pv(q_ref[q_rows, :].astype(BF16), k_ref[k_rows, :].astype(BF16),
                                 v_ref[k_rows, :].astype(BF16), valid)
            o_ref[q_rows, :] = o
            l_ref[q_rows, :] = jnp.broadcast_to(lse, (Q_BLOCK, HEAD_DIM))
        return carry

    lax.fori_loop(0, dil * n_blk // STRIDED_UNROLL, tiles, 0)


def _attn_strided(proj, gi):
    dil = DIL_GROUPS[gi][1]
    assert (SEQ // dil) % Q_BLOCK == 0 and (dil * (SEQ // dil // Q_BLOCK)) % STRIDED_UNROLL == 0
    heads_per_blk = PROJ_TN // HEAD_DIM
    src = lambda col: pl.BlockSpec(
        (SEQ, HEAD_DIM), lambda b, h: (b, (col + gi) * heads_per_blk + h))
    out = pl.BlockSpec((SEQ, HEAD_DIM), lambda b, h: (b, h))
    shape = jax.ShapeDtypeStruct((N_PROMPT, GROUP_WIDTH), F32)
    return pl.pallas_call(
        functools.partial(_attn_strided_kernel, dil),
        grid=(BATCH, HEADS_PER_GROUP),
        in_specs=[src(COL_Q), src(COL_K), src(COL_V)],
        out_specs=[out, out],
        out_shape=[shape, shape],
        compiler_params=_params(("parallel", "parallel")),
        name="attn_strided",
    )(proj, proj, proj)


def _attn_prompt_kernel(q_ref, k_ref, v_ref, o1_ref, l1_ref, o2_ref, l2_ref, o_ref):
    window = DIL_GROUPS[0][0]
    t0 = pl.multiple_of(pl.program_id(1) * Q_BLOCK, Q_BLOCK)
    n_keys = window + Q_BLOCK
    start = pl.multiple_of(jnp.maximum(t0 - window, 0), Q_BLOCK)
    dist = ((t0 - start) + lax.broadcasted_iota(jnp.int32, (Q_BLOCK, n_keys), 0)
            - lax.broadcasted_iota(jnp.int32, (Q_BLOCK, n_keys), 1))
    valid = jnp.where(dist >= 0, 1.0, 0.0) * jnp.where(dist <= window, 1.0, 0.0) > 0.5
    for h in range(HEADS_PER_GROUP):
        cs = slice(h * HEAD_DIM, (h + 1) * HEAD_DIM)
        o0, l0 = _softmax_pv(q_ref[:, cs].astype(BF16), k_ref[pl.ds(start, n_keys), cs].astype(BF16),
                             v_ref[pl.ds(start, n_keys), cs].astype(BF16), valid)
        l1, l2 = l1_ref[:, cs], l2_ref[:, cs]
        top = jnp.maximum(jnp.maximum(l0, l1), l2)
        w0, w1, w2 = jnp.exp(l0 - top), jnp.exp(l1 - top), jnp.exp(l2 - top)
        o_ref[:, cs] = (o0 * w0 + o1_ref[:, cs] * w1 + o2_ref[:, cs] * w2) / (w0 + w1 + w2)


def _attn_prompt(proj, strided):
    n_qb = SEQ // Q_BLOCK
    tile = pl.BlockSpec((Q_BLOCK, GROUP_WIDTH), lambda b, qb: (b * n_qb + qb, 0))
    (o1, l1), (o2, l2) = strided
    return pl.pallas_call(
        _attn_prompt_kernel,
        grid=(BATCH, n_qb),
        in_specs=[
            pl.BlockSpec((Q_BLOCK, GROUP_WIDTH), lambda b, qb: (b * n_qb + qb, COL_Q)),
            pl.BlockSpec((SEQ, GROUP_WIDTH), lambda b, qb: (b, COL_K)),
            pl.BlockSpec((SEQ, GROUP_WIDTH), lambda b, qb: (b, COL_V)),
            tile, tile, tile, tile,
        ],
        out_specs=tile,
        out_shape=jax.ShapeDtypeStruct((N_PROMPT, GROUP_WIDTH), F32),
        compiler_params=_params(("parallel", "arbitrary")),
        name="attn_prompt",
    )(proj, proj, proj, o1, l1, o2, l2)


SAMPLE_NB = 4


def _attn_sample_kernel(q_ref, kn_ref, vn_ref, k0_ref, v0_ref, k1_ref, v1_ref, k2_ref, v2_ref, o_ref):
    caches = ((k0_ref, v0_ref), (k1_ref, v1_ref), (k2_ref, v2_ref))

    def seq(n, carry):
        outs = []
        lses = []
        for g, (kc_ref, vc_ref) in enumerate(caches):
            q = q_ref[n, g]
            kc = kc_ref[n]
            s = jnp.sum(kc * q[None], axis=-1, keepdims=True) * ATTN_SCALE
            sn = jnp.sum(kn_ref[n, g] * q, axis=-1, keepdims=True) * ATTN_SCALE
            m = jnp.maximum(jnp.max(s, axis=0), sn)
            e = jnp.exp(s - m[None])
            en = jnp.exp(sn - m)
            den = jnp.sum(e, axis=0) + en
            o = (jnp.sum(e * vc_ref[n], axis=0) + en * vn_ref[n, g]) / den
            outs.append(o)
            lses.append(m + jnp.log(den))
        mx = jnp.maximum(jnp.maximum(lses[0], lses[1]), lses[2])
        ws = [jnp.exp(l - mx) for l in lses]
        tot = ws[0] + ws[1] + ws[2]
        o_ref[n] = (ws[0] * outs[0] + ws[1] * outs[1] + ws[2] * outs[2]) / tot
        return carry

    lax.fori_loop(0, SAMPLE_NB, seq, 0)


def _attn_sample(qs, kns, vns, caches):
    nb = SAMPLE_NB
    small = pl.BlockSpec((nb, len(DIL_GROUPS), HEADS_PER_GROUP, HEAD_DIM), lambda n: (n, 0, 0, 0))
    in_specs = [small, small, small]
    args = [qs, kns, vns]
    for c in caches:
        for kv in (0, 1):
            in_specs.append(pl.BlockSpec(
                (nb, N_KEYS, None, None, HEADS_PER_GROUP, HEAD_DIM),
                functools.partial(lambda n, kv_: (n, 0, 0, kv_, 0, 0), kv_=kv)))
            args.append(c)
    return pl.pallas_call(
        _attn_sample_kernel,
        grid=(DEC_BATCH // nb,),
        in_specs=in_specs,
        out_specs=pl.BlockSpec((nb, HEADS_PER_GROUP, HEAD_DIM), lambda n: (n, 0, 0)),
        out_shape=jax.ShapeDtypeStruct((DEC_BATCH, HEADS_PER_GROUP, HEAD_DIM), F32),
        compiler_params=_params(("parallel",)),
        name="attn_sample",
    )(*args)


MERGE_TM = 384


def _merge_kernel(x_ref, a_ref, b_ref, ga_ref, gb_ref, wa_ref, wb_ref, wo_ref, o_ref):
    pa = jnp.dot(a_ref[...].astype(BF16), wa_ref[...], preferred_element_type=F32)
    pb = jnp.dot(b_ref[...].astype(BF16), wb_ref[...], preferred_element_type=F32)
    merged = ga_ref[...] * pa + gb_ref[...] * pb
    o_ref[...] = x_ref[...] + jnp.dot(merged.astype(BF16), wo_ref[...], preferred_element_type=F32)


def _merge(x, amix, bmix, proj, wa, wb, wo):
    tm = MERGE_TM
    gate_a_blk = COL_GATE * PROJ_TN // D_MODEL
    return pl.pallas_call(
        _merge_kernel,
        grid=(N_PAD // tm,),
        in_specs=[
            pl.BlockSpec((tm, D_MODEL), lambda i: (i, 0)),
            pl.BlockSpec((tm, A_WIDTH), lambda i: (i, 0)),
            pl.BlockSpec((tm, GROUP_WIDTH), lambda i: (i, 0)),
            pl.BlockSpec((tm, D_MODEL), lambda i: (i, gate_a_blk)),
            pl.BlockSpec((tm, D_MODEL), lambda i: (i, gate_a_blk + 1)),
            _const_spec((A_WIDTH, D_MODEL)),
            _const_spec((GROUP_WIDTH, D_MODEL)),
            _const_spec((D_MODEL, D_MODEL)),
        ],
        out_specs=pl.BlockSpec((tm, D_MODEL), lambda i: (i, 0)),
        out_shape=jax.ShapeDtypeStruct((N_PAD, D_MODEL), F32),
        compiler_params=_params(("parallel",)),
        name="merge",
    )(x, amix, bmix, proj, proj, wa, wb, wo)


GATE_TB = 128
PAIRS_PER_RANK = tuple(PEER_TOPK // (a + 1) for a in range(PEER_TOPK))


SUBLANES = 8


def _batcher_pairs(n):
    pairs = []

    def merge(lo, m, r):
        step = 2 * r
        if step < m:
            merge(lo, m, step)
            merge(lo + r, m, step)
            pairs.extend((i, i + r) for i in range(lo + r, lo + m - r, step))
        else:
            pairs.append((lo, lo + r))

    def sort(lo, m):
        if m > 1:
            sort(lo, m // 2)
            sort(lo + m // 2, m // 2)
            merge(lo, m, 1)

    sort(0, n)
    return tuple(pairs)


SORT16_PAIRS = _batcher_pairs(PEER_TOPK)


def _exchange(v, i, j):
    v[i], v[j] = jnp.maximum(v[i], v[j]), jnp.minimum(v[i], v[j])


def _bitonic_sort_desc(c):
    c = list(c)
    d = PEER_TOPK // 2
    while d:
        for k in range(PEER_TOPK):
            if not k & d:
                _exchange(c, k, k + d)
        d //= 2
    return c


def _merge_sublanes(v, sort_last):
    for shift in (4, 2, 1):
        other = [pltpu.roll(x, shift, 0) for x in v]
        c = [jnp.maximum(v[k], other[PEER_TOPK - 1 - k]) for k in range(PEER_TOPK)]
        v = c if (shift == 1 and not sort_last) else _bitonic_sort_desc(c)
    return v


def _top16_sorted(s):
    v = [s[k * SUBLANES:(k + 1) * SUBLANES] for k in range(s.shape[0] // SUBLANES)]
    for i, j in SORT16_PAIRS:
        _exchange(v, i, j)
    return _merge_sublanes(v, True)


def _by_sublane(rows, sub8):
    out = rows[SUBLANES - 1]
    for b in range(SUBLANES - 2, -1, -1):
        out = jnp.where(sub8 == b, rows[b], out)
    return out


def _peer_gate_kernel(x_ref, g_ref, wq_ref, k1_ref, k2_ref,
                      xnt_ref, s1_ref, a_ref, s2_ref, bn_ref, tau_ref):
    xn = _rms(x_ref[...], g_ref[...])
    xnt_ref[...] = xn.T.astype(BF16)
    qp = jnp.dot(xn.astype(BF16), wq_ref[...], preferred_element_type=F32).astype(BF16)
    sub8 = lax.broadcasted_iota(jnp.int32, (8, GATE_TB), 0)
    nt = (((1,), (1,)), ((), ()))
    for h in range(PEER_HEADS):
        c0 = h * 2 * PEER_HALF
        s1 = lax.dot_general(k1_ref[...], qp[:, c0:c0 + PEER_HALF], nt, preferred_element_type=F32)
        s2 = lax.dot_general(k2_ref[...], qp[:, c0 + PEER_HALF:c0 + 2 * PEER_HALF], nt,
                             preferred_element_type=F32)
        t1 = _top16_sorted(s1)
        t2 = _top16_sorted(s2)
        t2_lo = _by_sublane(t2[:SUBLANES], sub8)
        t2_hi = _by_sublane(t2[SUBLANES:], sub8)
        lo = []
        for a in range(PEER_TOPK):
            nvalid = min(SUBLANES, PAIRS_PER_RANK[a])
            c = t1[a] + t2_lo
            lo.append(c if nvalid == SUBLANES else jnp.where(sub8 < nvalid, c, -jnp.inf))
        hi = t1[0] + t2_hi
        ins = [jnp.maximum(lo[0], hi)]
        ins += [jnp.maximum(lo[k], jnp.minimum(lo[k - 1], hi)) for k in range(1, PEER_TOPK)]
        best = _merge_sublanes(ins, False)
        tau = functools.reduce(jnp.minimum, best)
        e2_lo = jnp.exp(t2_lo - t2[0])
        zacc = jnp.where(hi >= tau, jnp.exp(t2_hi - t2[0]), 0.0)
        for a in range(PEER_TOPK):
            zacc = zacc + jnp.where(lo[a] >= tau, jnp.exp(t1[a] - t1[0]) * e2_lo, 0.0)
        z = jnp.sum(zacc, axis=0, keepdims=True)
        s1_ref[h] = s1
        s2_ref[h] = s2
        a_ref[h] = jnp.exp(s1 - t1[0][0:1])
        bn_ref[h] = jnp.exp(s2 - t2[0][0:1]) / z
        tau_ref[h] = tau[0:1]


def _peer_gate(x1, g, wq, k1, k2):
    tb = GATE_TB
    big = pl.BlockSpec((PEER_HEADS, N_KEYS, tb), lambda i: (0, 0, i))
    big_shape = jax.ShapeDtypeStruct((PEER_HEADS, N_KEYS, N_PAD), F32)
    return pl.pallas_call(
        _peer_gate_kernel,
        grid=(N_PAD // tb,),
        in_specs=[
            pl.BlockSpec((tb, D_MODEL), lambda i: (i, 0)),
            pl.BlockSpec((1, D_MODEL), lambda i: (0, 0)),
            _const_spec((D_MODEL, PEER_HEADS * 2 * PEER_HALF)),
            _const_spec((N_KEYS, PEER_HALF)),
            _const_spec((N_KEYS, PEER_HALF)),
        ],
        out_specs=[
            pl.BlockSpec((D_MODEL, tb), lambda i: (0, i)),
            big, big, big, big,
            pl.BlockSpec((PEER_HEADS, 1, tb), lambda i: (0, 0, i)),
        ],
        out_shape=[
            jax.ShapeDtypeStruct((D_MODEL, N_PAD), BF16),
            big_shape, big_shape, big_shape, big_shape,
            jax.ShapeDtypeStruct((PEER_HEADS, 1, N_PAD), F32),
        ],
        compiler_params=_params(("parallel",)),
        name="peer_gate",
    )(x1, g, wq, k1, k2)


DENSE_TB = 768
DENSE_EB = 1024
DENSE_EB_F32 = 512
N_DENSE_TOKEN_BLOCKS = N_PAD // DENSE_TB


def _peer_dense_kernel(from_f32, xnt_ref, s1_ref, a_ref, s2_ref, bn_ref, tau_ref, u_ref, v_ref, *rest):
    if from_f32:
        o_ref, ub_ref, vtb_ref, st_ref, wt_ref = rest
        u = u_ref[...].astype(BF16)
        v = v_ref[...].astype(BF16)
        ub_ref[...] = u
        vtb_ref[...] = v.T
    else:
        o_ref, st_ref, wt_ref = rest
    n_first = st_ref.shape[0] // N_KEYS

    @pl.when(pl.program_id(1) == 0)
    def _():
        o_ref[...] = jnp.zeros_like(o_ref)

    st_ref[...] = jnp.dot(u if from_f32 else u_ref[...], xnt_ref[...], preferred_element_type=F32)

    for ii in range(n_first):
        rows = slice(ii * N_KEYS, (ii + 1) * N_KEYS)
        for c in range(DENSE_TB // LANE):
            lanes = slice(c * LANE, (c + 1) * LANE)
            gate = jnp.zeros((N_KEYS, LANE), F32)
            for h in range(PEER_HEADS):
                s1_row = s1_ref[h, ii:ii + 1, lanes]
                a_row = a_ref[h, ii:ii + 1, lanes]
                hit = (s1_row + s2_ref[h, :, lanes]) >= tau_ref[h, :, lanes]
                gate = gate + jnp.where(hit, a_row * bn_ref[h, :, lanes], 0.0)
            act = jax.nn.gelu(st_ref[rows, lanes])
            wt_ref[rows, lanes] = (gate * act).astype(BF16)

    if from_f32:
        o_ref[...] += lax.dot_general(v, wt_ref[...], (((0,), (0,)), ((), ())), preferred_element_type=F32)
    else:
        o_ref[...] += jnp.dot(v_ref[...], wt_ref[...], preferred_element_type=F32)


def _peer_dense(xnt, s1t, at, s2t, bnt, tau, u_f32, v_f32):
    tb = DENSE_TB
    once = pl.Buffered(1)

    def token_specs(t_of):
        per_tok = lambda shape: pl.BlockSpec(shape, lambda t, e: (0,) * (len(shape) - 1) + (t_of(t),),
                                             pipeline_mode=once)
        return [per_tok((D_MODEL, tb)), per_tok((PEER_HEADS, N_KEYS, tb)), per_tok((PEER_HEADS, N_KEYS, tb)),
                per_tok((PEER_HEADS, 1, tb))]

    eb = DENSE_EB_F32
    ni = eb // N_KEYS
    n_e = N_EXPERTS // eb
    halves = DENSE_EB // eb
    first_keys = lambda a: a[:, :, :tb].reshape(PEER_HEADS, n_e, ni, tb)
    by_key0 = pl.BlockSpec((PEER_HEADS, None, ni, tb), lambda t, e: (0, e, 0, 0))
    xnt_s, s2_s, bn_s, tau_s = token_specs(lambda t: 0)
    out0, u_bf, vt_bf = pl.pallas_call(
        functools.partial(_peer_dense_kernel, True),
        grid=(1, n_e),
        in_specs=[xnt_s, by_key0, by_key0, s2_s, bn_s, tau_s,
                  pl.BlockSpec((eb, D_MODEL), lambda t, e: (e, 0)),
                  pl.BlockSpec((eb, D_MODEL), lambda t, e: (e, 0))],
        out_specs=[
            pl.BlockSpec((D_MODEL, tb), lambda t, e: (0, 0), pipeline_mode=once),
            pl.BlockSpec((eb, D_MODEL), lambda t, e: (e, 0)),
            pl.BlockSpec((None, D_MODEL, eb), lambda t, e: (e // halves, 0, e % halves)),
        ],
        out_shape=[
            jax.ShapeDtypeStruct((D_MODEL, tb), F32),
            jax.ShapeDtypeStruct((N_EXPERTS, D_MODEL), BF16),
            jax.ShapeDtypeStruct((N_EXPERTS // DENSE_EB, D_MODEL, DENSE_EB), BF16),
        ],
        scratch_shapes=[pltpu.VMEM((eb, tb), F32), pltpu.VMEM((eb, tb), BF16)],
        compiler_params=_params(("arbitrary", "arbitrary")),
        name="peer_dense_first",
    )(xnt, first_keys(s1t), first_keys(at), s2t, bnt, tau, u_f32, v_f32)

    eb = DENSE_EB
    by_key = pl.BlockSpec((PEER_HEADS, eb // N_KEYS, tb), lambda t, e: (0, e, t + 1))
    xnt_s, s2_s, bn_s, tau_s = token_specs(lambda t: t + 1)
    out_rest = pl.pallas_call(
        functools.partial(_peer_dense_kernel, False),
        grid=(N_DENSE_TOKEN_BLOCKS - 1, N_EXPERTS // eb),
        in_specs=[xnt_s, by_key, by_key, s2_s, bn_s, tau_s,
                  pl.BlockSpec((eb, D_MODEL), lambda t, e: (e, 0)),
                  pl.BlockSpec((None, D_MODEL, eb), lambda t, e: (e, 0, 0))],
        out_specs=pl.BlockSpec((D_MODEL, tb), lambda t, e: (0, t)),
        out_shape=jax.ShapeDtypeStruct((D_MODEL, N_PAD - tb), F32),
        scratch_shapes=[pltpu.VMEM((eb, tb), F32), pltpu.VMEM((eb, tb), BF16)],
        compiler_params=_params(("parallel", "arbitrary")),
        name="peer_dense",
    )(xnt, s1t, at, s2t, bnt, tau, u_bf, vt_bf)
    return out0, out_rest


FINAL_TM = 256
N_PROMPT_FINAL_BLOCKS = N_PROMPT // FINAL_TM


FINAL_FIRST_BLOCKS = DENSE_TB // FINAL_TM


def _final_kernel(x_ref, pt0_ref, pt_ref, p_ref, wg_ref, wp_ref, g_ref, yp_ref, ys_ref):
    i = pl.program_id(0)
    peer_t = jnp.where(i < FINAL_FIRST_BLOCKS, pt0_ref[...], pt_ref[...])
    x2 = x_ref[...] + peer_t.T
    gate = _sigmoid(jnp.dot(x2.astype(BF16), wg_ref[...], preferred_element_type=F32))
    emb = jnp.dot(p_ref[...].astype(BF16), wp_ref[...], preferred_element_type=F32)
    y = _rms(x2 + gate * emb, g_ref[...])

    @pl.when(i < N_PROMPT_FINAL_BLOCKS)
    def _():
        yp_ref[...] = y

    @pl.when(i == N_PROMPT_FINAL_BLOCKS)
    def _():
        ys_ref[...] = y[:DEC_BATCH]


def _final(x1, peer_t0, peer_t, p, wg, wp, g):
    tm = FINAL_TM
    assert N_PAD // tm == N_PROMPT_FINAL_BLOCKS + 1 and DENSE_TB % tm == 0
    nf = FINAL_FIRST_BLOCKS
    return pl.pallas_call(
        _final_kernel,
        grid=(N_PAD // tm,),
        in_specs=[
            pl.BlockSpec((tm, D_MODEL), lambda i: (i, 0)),
            pl.BlockSpec((D_MODEL, tm), lambda i: (0, jnp.minimum(i, nf - 1))),
            pl.BlockSpec((D_MODEL, tm), lambda i: (0, jnp.maximum(i - nf, 0))),
            pl.BlockSpec((tm, PLE_DIM), lambda i: (i, 0)),
            _const_spec((D_MODEL, D_MODEL)),
            _const_spec((PLE_DIM, D_MODEL)),
            pl.BlockSpec((1, D_MODEL), lambda i: (0, 0)),
        ],
        out_specs=[
            pl.BlockSpec((tm, D_MODEL), lambda i: (jnp.minimum(i, N_PROMPT_FINAL_BLOCKS - 1), 0)),
            pl.BlockSpec((DEC_BATCH, D_MODEL), lambda i: (0, 0)),
        ],
        out_shape=[
            jax.ShapeDtypeStruct((N_PROMPT, D_MODEL), F32),
            jax.ShapeDtypeStruct((DEC_BATCH, D_MODEL), F32),
        ],
        compiler_params=_params(("arbitrary",)),
        name="final",
    )(x1, peer_t0, peer_t, p, wg, wp, g)


KV_ROWS = 512


def _kv_store(k, v, o_ref):
    for h in range(HEADS_PER_GROUP):
        cs = slice(h * HEAD_DIM, (h + 1) * HEAD_DIM)
        o_ref[:, 0, h, :] = k[:, cs]
        o_ref[:, 1, h, :] = v[:, cs]


def _kv_prompt_kernel(k_ref, v_ref, o_ref):
    _kv_store(k_ref[...], v_ref[...], o_ref)


def _kv_prompt(proj, gi, rows):
    blk = min(rows, KV_ROWS)
    nblk = rows // blk
    first = SEQ // blk - nblk
    src = lambda col: pl.BlockSpec(
        (blk, GROUP_WIDTH), lambda b, r: (b * (SEQ // blk) + first + r, col + gi))
    return pl.pallas_call(
        _kv_prompt_kernel,
        grid=(BATCH, nblk),
        in_specs=[src(COL_K), src(COL_V)],
        out_specs=pl.BlockSpec((None, None, blk, 2, HEADS_PER_GROUP, HEAD_DIM),
                               lambda b, r: (0, b, r, 0, 0, 0)),
        out_shape=jax.ShapeDtypeStruct((1, BATCH, rows, 2, HEADS_PER_GROUP, HEAD_DIM), F32),
        compiler_params=_params(("parallel", "parallel")),
        name="kv_prompt",
    )(proj, proj)


def _kv_sample_kernel(k0_ref, v0_ref, k1_ref, v1_ref, k2_ref, v2_ref, o0_ref, o1_ref, o2_ref):
    _kv_store(k0_ref[...], v0_ref[...], o0_ref)
    _kv_store(k1_ref[...], v1_ref[...], o1_ref)
    _kv_store(k2_ref[...], v2_ref[...], o2_ref)


def _kv_sample(proj):
    row_blk = N_PROMPT // DEC_BATCH
    src = lambda col: pl.BlockSpec((DEC_BATCH, GROUP_WIDTH), lambda i: (row_blk, col))
    in_specs = []
    for gi in range(len(DIL_GROUPS)):
        in_specs += [src(COL_K + gi), src(COL_V + gi)]
    out_spec = pl.BlockSpec((None, DEC_BATCH, None, 2, HEADS_PER_GROUP, HEAD_DIM),
                            lambda i: (0, 0, 0, 0, 0, 0))
    out_shape = jax.ShapeDtypeStruct((1, DEC_BATCH, 1, 2, HEADS_PER_GROUP, HEAD_DIM), F32)
    return pl.pallas_call(
        _kv_sample_kernel,
        grid=(1,),
        in_specs=in_specs,
        out_specs=[out_spec] * 3,
        out_shape=[out_shape] * 3,
        compiler_params=_params(("arbitrary",)),
        name="kv_sample",
    )(*([proj] * 6))


def _rope_tables():
    pos = jnp.concatenate([
        jnp.tile(jnp.arange(SEQ, dtype=jnp.int32), BATCH),
        jnp.full((DEC_BATCH,), PAST_LEN, jnp.int32),
        jnp.zeros((N_PAD - N_TOK,), jnp.int32)])
    inv = ROPE_THETA ** (-jnp.arange(ROT_HALF, dtype=F32) / ROT_HALF)
    ang = pos.astype(F32)[:, None] * inv[None, :]
    cos, sin = jnp.cos(ang), jnp.sin(ang)
    n = pos.shape[0]
    zeros = lambda w: jnp.zeros((n, w), F32)
    c = jnp.concatenate([cos, cos, jnp.ones((n, HEAD_DIM - 2 * ROT_HALF), F32)], axis=1)
    s_hi = jnp.concatenate([-sin, zeros(HEAD_DIM - ROT_HALF)], axis=1)
    s_lo = jnp.concatenate([zeros(ROT_HALF), sin, zeros(HEAD_DIM - 2 * ROT_HALF)], axis=1)
    return c, s_hi, s_lo


def kernel(x_prompt, x_sample, cache_kv_w128, cache_kv_w512, cache_kv_w2048, p_prompt, p_sample, g_mix, w_in, sgu_ln_g, sgu_ln_b, w_s, b_s, w_a_out, w_b_out, w_o, g_ffn, peer_w_q, peer_sub_k1, peer_sub_k2, peer_u, peer_v, w_ple, w_ple_gate, g_final):
    assert x_prompt.shape == (BATCH, SEQ, D_MODEL) and x_sample.shape == (DEC_BATCH, 1, D_MODEL)
    assert w_in.shape == (1, D_MODEL, IN_COLS)
    pad = N_PAD - N_TOK
    x = jnp.concatenate([x_prompt.reshape(N_PROMPT, D_MODEL), x_sample.reshape(DEC_BATCH, D_MODEL),
                         jnp.zeros((pad, D_MODEL), F32)], axis=0)
    p = jnp.concatenate([p_prompt.reshape(N_PROMPT, PLE_DIM), p_sample.reshape(DEC_BATCH, PLE_DIM),
                         jnp.zeros((pad, PLE_DIM), F32)], axis=0)

    c, s_hi, s_lo = _rope_tables()
    proj = _proj(x, g_mix, w_in[0].astype(BF16), c, s_hi, s_lo)

    bsb = jnp.broadcast_to(b_s[0][:, :, None], (A_GROUPS, CHUNK, CHUNK))
    wd = jnp.repeat(w_s[0, :, 0, 0], CHUNK)[None]
    b0 = jnp.repeat(b_s[0, :, 0], CHUNK)[None]
    amix, vn_s = _sgu(proj, sgu_ln_g, sgu_ln_b, w_s[0], bsb, wd, b0)

    bmix_p = _attn_prompt(proj, [_attn_strided(proj, gi) for gi in STRIDED_GROUPS])
    cq, ck, cv = COL_Q * PROJ_TN, COL_K * PROJ_TN, COL_V * PROJ_TN
    srows = proj[N_PROMPT:N_TOK]
    per_head = lambda a: a.reshape(DEC_BATCH, len(DIL_GROUPS), HEADS_PER_GROUP, HEAD_DIM)
    caches = []
    for cache, (window, dil) in zip((cache_kv_w128, cache_kv_w512, cache_kv_w2048), DIL_GROUPS):
        l_buf = cache.shape[2]
        assert l_buf == N_KEYS * dil
        caches.append(cache.reshape(DEC_BATCH, N_KEYS, dil, 2, HEADS_PER_GROUP, HEAD_DIM))
    bmix_s = _attn_sample(per_head(srows[:, cq:ck]), per_head(srows[:, ck:cv]), per_head(srows[:, cv:cv + B_WIDTH]), caches)
    bmix = jnp.concatenate([bmix_p, bmix_s.reshape(DEC_BATCH, GROUP_WIDTH),
                            jnp.zeros((pad, GROUP_WIDTH), F32)], axis=0)

    x1 = _merge(x, amix, bmix, proj, w_a_out[0].astype(BF16), w_b_out[0].astype(BF16), w_o[0].astype(BF16))

    xnt, s1t, at, s2t, bnt, tau = _peer_gate(x1, g_ffn, peer_w_q[0].astype(BF16),
                                             peer_sub_k1[0].astype(BF16), peer_sub_k2[0].astype(BF16))
    peer_t0, peer_t = _peer_dense(xnt, s1t, at, s2t, bnt, tau, peer_u[0], peer_v[0])

    y_p, y_s = _final(x1, peer_t0, peer_t, p, w_ple_gate[0].astype(BF16), w_ple[0].astype(BF16), g_final[None])

    y_prompt = y_p.reshape(BATCH, SEQ, D_MODEL)
    y_sample = y_s.reshape(DEC_BATCH, 1, D_MODEL)
    kv_p = [_kv_prompt(proj, gi, min(window, SEQ)) for gi, (window, _) in enumerate(DIL_GROUPS)]
    kv_s = _kv_sample(proj)
    sgu_v_sample = vn_s.reshape(1, DEC_BATCH, 1, A_WIDTH)
    return (y_prompt, y_sample, kv_p[0], kv_p[1], kv_p[2], kv_s[0], kv_s[1], kv_s[2], sgu_v_sample)
```

```python
import functools
import math

import jax
import jax.numpy as jnp
from jax import lax
from jax.experimental import pallas as pl
from jax.experimental.pallas import tpu as pltpu

F32 = jnp.float32
BF16 = jnp.bfloat16

D_MODEL = 2048
BATCH = 4
SEQ = 2048
DEC_BATCH = 128
PAST_LEN = 2048
EPS = 1e-6
CHUNK = 128
A_GROUPS = 8
A_WIDTH = 1024
HEAD_DIM = 128
HEADS_PER_GROUP = 4
DIL_GROUPS = ((128, 1), (512, 4), (2048, 16))
GROUP_WIDTH = HEADS_PER_GROUP * HEAD_DIM
B_WIDTH = 3 * GROUP_WIDTH
ATTN_SCALE = HEAD_DIM ** -0.5
ROPE_THETA = 500000.0
ROT_HALF = HEAD_DIM // 8
IN_COLS = 2 * A_WIDTH + 3 * B_WIDTH + 2 * D_MODEL
N_KEYS = 128
N_EXPERTS = N_KEYS * N_KEYS
PEER_HEADS = 8
PEER_HALF = 128
PEER_TOPK = 16
PLE_DIM = 256

N_PROMPT = BATCH * SEQ
N_TOK = N_PROMPT + DEC_BATCH
N_PAD = 8448
LANE = 128

PROJ_TN = 512
COL_GATE = 2 * A_WIDTH // PROJ_TN
COL_Q = COL_GATE + 2 * D_MODEL // PROJ_TN
COL_K = COL_Q + B_WIDTH // PROJ_TN
COL_V = COL_K + B_WIDTH // PROJ_TN
N_COL_BLOCKS = IN_COLS // PROJ_TN

VMEM_LIMIT = 56 * 1024 * 1024


def _params(sem, vmem=VMEM_LIMIT):
    return pltpu.CompilerParams(dimension_semantics=sem, vmem_limit_bytes=vmem)


def _const_spec(shape):
    nd = len(shape)
    return pl.BlockSpec(shape, lambda *_: (0,) * nd, pipeline_mode=pl.Buffered(1))


def _rms(x, g):
    r = lax.rsqrt(jnp.mean(x * x, axis=-1, keepdims=True) + EPS)
    return (x * r) * g


def _sigmoid(x):
    return 1.0 / (1.0 + jnp.exp(-x))


PROJ_TM = 1024
PROJ_CHUNK = 256


PROJ_STEP_COLS = 2 * PROJ_TN
PROJ_COLS = -(-IN_COLS // PROJ_STEP_COLS) * PROJ_STEP_COLS


def _proj_kernel(x_ref, g_ref, wa_ref, wb_ref, c_ref, s1_ref, s2_ref, o_ref, h_ref):
    j = 2 * pl.program_id(1)

    @pl.when(pl.program_id(1) == 0)
    def _():
        h_ref[...] = _rms(x_ref[...], g_ref[...]).astype(BF16)

    def rope(a):
        c = c_ref[...]
        s1 = s1_ref[...]
        s2 = s2_ref[...]
        heads = []
        for hh in range(a.shape[1] // HEAD_DIM):
            ah = a[:, hh * HEAD_DIM:(hh + 1) * HEAD_DIM]
            heads.append(ah * c + pltpu.roll(ah, HEAD_DIM - ROT_HALF, 1) * s1 + pltpu.roll(ah, ROT_HALF, 1) * s2)
        return jnp.concatenate(heads, axis=1)

    def project(epilogue):
        def branch():
            for half, w_ref in enumerate((wa_ref, wb_ref)):
                for c0 in range(0, PROJ_TN, PROJ_CHUNK):
                    acc = jnp.dot(h_ref[...], w_ref[:, c0:c0 + PROJ_CHUNK], preferred_element_type=F32)
                    o0 = half * PROJ_TN + c0
                    o_ref[:, o0:o0 + PROJ_CHUNK] = epilogue(acc)
        return branch

    pl.when(j < COL_GATE)(project(jax.nn.gelu))
    pl.when((j >= COL_GATE) & (j < COL_Q))(project(_sigmoid))
    pl.when((j >= COL_Q) & (j < COL_V))(project(rope))
    pl.when(j >= COL_V)(project(lambda a: a))


def _w_in_col_block(j):
    n_qkv = 3 * B_WIDTH // PROJ_TN
    n_gate = 2 * D_MODEL // PROJ_TN
    j = jnp.minimum(j, N_COL_BLOCKS - 1)
    return jnp.where(j < COL_GATE, j, jnp.where(j < COL_Q, j + n_qkv, j - n_gate))


def _proj_rows(x_rows, g, w, c, s1, s2, tm, first_blk, filled=None):
    n_blk = x_rows.shape[0] // tm
    w_half = lambda half: pl.BlockSpec((D_MODEL, PROJ_TN), lambda i, j: (0, _w_in_col_block(2 * j + half)))
    table = pl.BlockSpec((tm, HEAD_DIM), lambda i, j: (i + first_blk, 0))
    in_specs = [
        pl.BlockSpec((tm, D_MODEL), lambda i, j: (i, 0)),
        pl.BlockSpec((1, D_MODEL), lambda i, j: (0, 0)),
        w_half(0),
        w_half(1),
        table, table, table,
    ]
    args = [x_rows, g, w, w, c, s1, s2]
    kernel_fn, aliases = _proj_kernel, {}
    if filled is not None:
        in_specs.append(pl.BlockSpec(memory_space=pl.ANY))
        args.append(filled)
        aliases = {len(args) - 1: 0}
        kernel_fn = lambda *refs: _proj_kernel(*refs[:7], *refs[8:])
    return pl.pallas_call(
        kernel_fn,
        grid=(n_blk, PROJ_COLS // PROJ_STEP_COLS),
        in_specs=in_specs,
        out_specs=pl.BlockSpec((tm, PROJ_STEP_COLS), lambda i, j: (i + first_blk, j)),
        out_shape=jax.ShapeDtypeStruct((N_PAD, PROJ_COLS), F32),
        scratch_shapes=[pltpu.VMEM((tm, D_MODEL), BF16)],
        input_output_aliases=aliases,
        compiler_params=_params(("parallel", "arbitrary")),
        name="proj",
    )(*args)


def _proj(x_prompt, x_tail, g, w, c, s1, s2):
    tail = x_tail.shape[0]
    assert N_PROMPT % PROJ_TM == 0 and N_PROMPT % tail == 0 and N_PROMPT + tail == N_PAD
    proj = _proj_rows(x_prompt, g, w, c, s1, s2, PROJ_TM, 0)
    return _proj_rows(x_tail, g, w, c, s1, s2, tail, N_PROMPT // tail, filled=proj)


N_PROMPT_CHUNKS = N_PROMPT // CHUNK


def _sgu_kernel(u_ref, gv_ref, lng_ref, lnb_ref, ws_ref, bsb_ref, wd_ref, b0_ref, o_ref, vn_ref):
    i = pl.program_id(0)
    gv = gv_ref[...]
    mu = jnp.mean(gv, axis=-1, keepdims=True)
    var = jnp.mean(jnp.square(gv - mu), axis=-1, keepdims=True)
    vn = ((gv - mu) * lax.rsqrt(var + EPS)) * lng_ref[...] + lnb_ref[...]

    @pl.when(i < N_PROMPT_CHUNKS)
    def _():
        row = lax.broadcasted_iota(jnp.int32, (CHUNK, CHUNK), 0)
        col = lax.broadcasted_iota(jnp.int32, (CHUNK, CHUNK), 1)
        causal = col <= row
        for g in range(A_GROUPS):
            cs = slice(g * CHUNK, (g + 1) * CHUNK)
            w = jnp.where(causal, ws_ref[g], 0.0).astype(BF16)
            mixed = jnp.dot(w, vn[:, cs].astype(BF16), preferred_element_type=F32) + bsb_ref[g]
            o_ref[:, cs] = u_ref[:, cs] * mixed

    @pl.when(i >= N_PROMPT_CHUNKS)
    def _():
        o_ref[...] = u_ref[...] * (vn * wd_ref[...] + b0_ref[...])

    @pl.when(i == N_PROMPT_CHUNKS)
    def _():
        vn_ref[...] = vn


def _sgu(proj, lng, lnb, ws, bsb, wd, b0):
    return pl.pallas_call(
        _sgu_kernel,
        grid=(N_PAD // CHUNK,),
        in_specs=[
            pl.BlockSpec((CHUNK, A_WIDTH), lambda i: (i, 0)),
            pl.BlockSpec((CHUNK, A_WIDTH), lambda i: (i, 1)),
            pl.BlockSpec((1, A_WIDTH), lambda i: (0, 0)),
            pl.BlockSpec((1, A_WIDTH), lambda i: (0, 0)),
            pl.BlockSpec((A_GROUPS, CHUNK, CHUNK), lambda i: (0, 0, 0)),
            pl.BlockSpec((A_GROUPS, CHUNK, CHUNK), lambda i: (0, 0, 0)),
            pl.BlockSpec((1, A_WIDTH), lambda i: (0, 0)),
            pl.BlockSpec((1, A_WIDTH), lambda i: (0, 0)),
        ],
        out_specs=[
            pl.BlockSpec((CHUNK, A_WIDTH), lambda i: (i, 0)),
            pl.BlockSpec((DEC_BATCH, A_WIDTH), lambda i: (0, 0)),
        ],
        out_shape=[
            jax.ShapeDtypeStruct((N_PAD, A_WIDTH), F32),
            jax.ShapeDtypeStruct((DEC_BATCH, A_WIDTH), F32),
        ],
        compiler_params=_params(("arbitrary",)),
        name="sgu",
    )(proj, proj, lng, lnb, ws, bsb, wd, b0)


Q_BLOCK = 128


def _softmax_pv(q, k, v, valid):
    s = lax.dot_general(q, k, (((1,), (1,)), ((), ())), preferred_element_type=F32) * ATTN_SCALE
    s = jnp.where(valid, s, -jnp.inf)
    m = jnp.max(s, axis=-1, keepdims=True)
    e = jnp.exp(s - m)
    den = jnp.sum(e, axis=-1, keepdims=True)
    o = jnp.dot((e / den).astype(BF16), v, preferred_element_type=F32)
    return o, m + jnp.log(den)


STRIDED_GROUPS = tuple(gi for gi, (_, d) in enumerate(DIL_GROUPS) if d > 1)
assert STRIDED_GROUPS == (1, 2) and all(w // d == Q_BLOCK for w, d in DIL_GROUPS)
STRIDED_UNROLL = 4


def _attn_strided_kernel(dil, q_ref, k_ref, v_ref, o_ref, l_ref):
    n_blk = SEQ // dil // Q_BLOCK
    n_keys = Q_BLOCK * min(n_blk, 2)
    dist0 = (lax.broadcasted_iota(jnp.int32, (Q_BLOCK, n_keys), 0)
             - lax.broadcasted_iota(jnp.int32, (Q_BLOCK, n_keys), 1))

    def tiles(it, carry):
        for u in range(STRIDED_UNROLL):
            tile = it * STRIDED_UNROLL + u
            r, n = tile // n_blk, tile % n_blk
            if n_blk == STRIDED_UNROLL:
                r, n = it, u
            first_key_blk = jnp.maximum(n - 1, 0)
            q_rows = pl.ds(r + dil * Q_BLOCK * n, Q_BLOCK, stride=dil)
            k_rows = pl.ds(r + dil * Q_BLOCK * first_key_blk, n_keys, stride=dil)
            dist = dist0 + Q_BLOCK * (n - first_key_blk)
            valid = jnp.where(dist >= 0, 1.0, 0.0) * jnp.where(dist <= Q_BLOCK, 1.0, 0.0) > 0.5
            o, lse = _softmax_pv(q_ref[q_rows, :].astype(BF16), k_ref[k_rows, :].astype(BF16),
                                 v_ref[k_rows, :].astype(BF16), valid)
            o_ref[q_rows, :] = o
            l_ref[q_rows, :] = jnp.broadcast_to(lse, (Q_BLOCK, HEAD_DIM))
        return carry

    lax.fori_loop(0, dil * n_blk // STRIDED_UNROLL, tiles, 0)


def _attn_strided(proj, gi):
    dil = DIL_GROUPS[gi][1]
    assert (SEQ // dil) % Q_BLOCK == 0 and (dil * (SEQ // dil // Q_BLOCK)) % STRIDED_UNROLL == 0
    heads_per_blk = PROJ_TN // HEAD_DIM
    src = lambda col: pl.BlockSpec(
        (SEQ, HEAD_DIM), lambda b, h: (b, (col + gi) * heads_per_blk + h))
    out = pl.BlockSpec((SEQ, HEAD_DIM), lambda b, h: (b, h))
    shape = jax.ShapeDtypeStruct((N_PROMPT, GROUP_WIDTH), F32)
    return pl.pallas_call(
        functools.partial(_attn_strided_kernel, dil),
        grid=(BATCH, HEADS_PER_GROUP),
        in_specs=[src(COL_Q), src(COL_K), src(COL_V)],
        out_specs=[out, out],
        out_shape=[shape, shape],
        compiler_params=_params(("parallel", "parallel")),
        name="attn_strided",
    )(proj, proj, proj)


def _attn_prompt_kernel(q_ref, k_ref, v_ref, o1_ref, l1_ref, o2_ref, l2_ref, o_ref):
    window = DIL_GROUPS[0][0]
    t0 = pl.multiple_of(pl.program_id(1) * Q_BLOCK, Q_BLOCK)
    n_keys = window + Q_BLOCK
    start = pl.multiple_of(jnp.maximum(t0 - window, 0), Q_BLOCK)
    dist = ((t0 - start) + lax.broadcasted_iota(jnp.int32, (Q_BLOCK, n_keys), 0)
            - lax.broadcasted_iota(jnp.int32, (Q_BLOCK, n_keys), 1))
    valid = jnp.where(dist >= 0, 1.0, 0.0) * jnp.where(dist <= window, 1.0, 0.0) > 0.5
    for h in range(HEADS_PER_GROUP):
        cs = slice(h * HEAD_DIM, (h + 1) * HEAD_DIM)
        o0, l0 = _softmax_pv(q_ref[:, cs].astype(BF16), k_ref[pl.ds(start, n_keys), cs].astype(BF16),
                             v_ref[pl.ds(start, n_keys), cs].astype(BF16), valid)
        l1, l2 = l1_ref[:, cs], l2_ref[:, cs]
        top = jnp.maximum(jnp.maximum(l0, l1), l2)
        w0, w1, w2 = jnp.exp(l0 - top), jnp.exp(l1 - top), jnp.exp(l2 - top)
        o_ref[:, cs] = (o0 * w0 + o1_ref[:, cs] * w1 + o2_ref[:, cs] * w2) / (w0 + w1 + w2)


def _attn_prompt(proj, strided):
    n_qb = SEQ // Q_BLOCK
    tile = pl.BlockSpec((Q_BLOCK, GROUP_WIDTH), lambda b, qb: (b * n_qb + qb, 0))
    (o1, l1), (o2, l2) = strided
    return pl.pallas_call(
        _attn_prompt_kernel,
        grid=(BATCH, n_qb),
        in_specs=[
            pl.BlockSpec((Q_BLOCK, GROUP_WIDTH), lambda b, qb: (b * n_qb + qb, COL_Q)),
            pl.BlockSpec((SEQ, GROUP_WIDTH), lambda b, qb: (b, COL_K)),
            pl.BlockSpec((SEQ, GROUP_WIDTH), lambda b, qb: (b, COL_V)),
            tile, tile, tile, tile,
        ],
        out_specs=tile,
        out_shape=jax.ShapeDtypeStruct((N_PROMPT, GROUP_WIDTH), F32),
        compiler_params=_params(("parallel", "arbitrary")),
        name="attn_prompt",
    )(proj, proj, proj, o1, l1, o2, l2)


SAMPLE_NB = 4


def _attn_sample_kernel(q_ref, kn_ref, vn_ref, k0_ref, v0_ref, k1_ref, v1_ref, k2_ref, v2_ref, o_ref):
    caches = ((k0_ref, v0_ref), (k1_ref, v1_ref), (k2_ref, v2_ref))

    def seq(n, carry):
        outs = []
        lses = []
        for g, (kc_ref, vc_ref) in enumerate(caches):
            q = q_ref[n, g]
            kc = kc_ref[n]
            s = jnp.sum(kc * q[None], axis=-1, keepdims=True) * ATTN_SCALE
            sn = jnp.sum(kn_ref[n, g] * q, axis=-1, keepdims=True) * ATTN_SCALE
            m = jnp.maximum(jnp.max(s, axis=0), sn)
            e = jnp.exp(s - m[None])
            en = jnp.exp(sn - m)
            den = jnp.sum(e, axis=0) + en
            o = (jnp.sum(e * vc_ref[n], axis=0) + en * vn_ref[n, g]) / den
            outs.append(o)
            lses.append(m + jnp.log(den))
        mx = jnp.maximum(jnp.maximum(lses[0], lses[1]), lses[2])
        ws = [jnp.exp(l - mx) for l in lses]
        tot = ws[0] + ws[1] + ws[2]
        o_ref[n] = (ws[0] * outs[0] + ws[1] * outs[1] + ws[2] * outs[2]) / tot
        return carry

    lax.fori_loop(0, SAMPLE_NB, seq, 0)


def _attn_sample(qs, kns, vns, caches):
    nb = SAMPLE_NB
    small = pl.BlockSpec((nb, len(DIL_GROUPS), HEADS_PER_GROUP, HEAD_DIM), lambda n: (n, 0, 0, 0))
    in_specs = [small, small, small]
    args = [qs, kns, vns]
    for c in caches:
        for kv in (0, 1):
            in_specs.append(pl.BlockSpec(
                (nb, N_KEYS, None, None, HEADS_PER_GROUP, HEAD_DIM),
                functools.partial(lambda n, kv_: (n, 0, 0, kv_, 0, 0), kv_=kv)))
            args.append(c)
    return pl.pallas_call(
        _attn_sample_kernel,
        grid=(DEC_BATCH // nb,),
        in_specs=in_specs,
        out_specs=pl.BlockSpec((nb, HEADS_PER_GROUP, HEAD_DIM), lambda n: (n, 0, 0)),
        out_shape=jax.ShapeDtypeStruct((DEC_BATCH, HEADS_PER_GROUP, HEAD_DIM), F32),
        compiler_params=_params(("parallel",)),
        name="attn_sample",
    )(*args)


TAIL_ROWS = N_PAD - N_PROMPT
MERGE_TM = TAIL_ROWS
N_PROMPT_BLOCKS = N_PROMPT // TAIL_ROWS


def _prompt_or_tail_specs(width):
    return [pl.BlockSpec((TAIL_ROWS, width), lambda i: (jnp.minimum(i, N_PROMPT_BLOCKS - 1), 0)),
            pl.BlockSpec((TAIL_ROWS, width), lambda i: (0, 0))]


def _prompt_or_tail(prompt_ref, tail_ref):
    return jnp.where(pl.program_id(0) < N_PROMPT_BLOCKS, prompt_ref[...], tail_ref[...])


def _merge_kernel(xp_ref, xt_ref, a_ref, bp_ref, bt_ref, ga_ref, gb_ref, wa_ref, wb_ref, wo_ref, o_ref):
    pa = jnp.dot(a_ref[...].astype(BF16), wa_ref[...], preferred_element_type=F32)
    pb = jnp.dot(_prompt_or_tail(bp_ref, bt_ref).astype(BF16), wb_ref[...], preferred_element_type=F32)
    merged = ga_ref[...] * pa + gb_ref[...] * pb
    o_ref[...] = _prompt_or_tail(xp_ref, xt_ref) + jnp.dot(merged.astype(BF16), wo_ref[...],
                                                          preferred_element_type=F32)


def _merge(x_prompt, x_tail, amix, bmix_prompt, bmix_tail, proj, wa, wb, wo):
    tm = MERGE_TM
    gate_a_blk = COL_GATE * PROJ_TN // D_MODEL
    return pl.pallas_call(
        _merge_kernel,
        grid=(N_PAD // tm,),
        in_specs=[
            *_prompt_or_tail_specs(D_MODEL),
            pl.BlockSpec((tm, A_WIDTH), lambda i: (i, 0)),
            *_prompt_or_tail_specs(GROUP_WIDTH),
            pl.BlockSpec((tm, D_MODEL), lambda i: (i, gate_a_blk)),
            pl.BlockSpec((tm, D_MODEL), lambda i: (i, gate_a_blk + 1)),
            _const_spec((A_WIDTH, D_MODEL)),
            _const_spec((GROUP_WIDTH, D_MODEL)),
            _const_spec((D_MODEL, D_MODEL)),
        ],
        out_specs=pl.BlockSpec((tm, D_MODEL), lambda i: (i, 0)),
        out_shape=jax.ShapeDtypeStruct((N_PAD, D_MODEL), F32),
        compiler_params=_params(("parallel",)),
        name="merge",
    )(x_prompt, x_tail, amix, bmix_prompt, bmix_tail, proj, proj, wa, wb, wo)


GATE_TB = 128
PAIRS_PER_RANK = tuple(PEER_TOPK // (a + 1) for a in range(PEER_TOPK))


SUBLANES = 8


def _batcher_pairs(n):
    pairs = []

    def merge(lo, m, r):
        step = 2 * r
        if step < m:
            merge(lo, m, step)
            merge(lo + r, m, step)
            pairs.extend((i, i + r) for i in range(lo + r, lo + m - r, step))
        else:
            pairs.append((lo, lo + r))

    def sort(lo, m):
        if m > 1:
            sort(lo, m // 2)
            sort(lo + m // 2, m // 2)
            merge(lo, m, 1)

    sort(0, n)
    return tuple(pairs)


SORT16_PAIRS = _batcher_pairs(PEER_TOPK)


def _exchange(v, i, j):
    v[i], v[j] = jnp.maximum(v[i], v[j]), jnp.minimum(v[i], v[j])


def _bitonic_sort_desc(c):
    c = list(c)
    d = PEER_TOPK // 2
    while d:
        for k in range(PEER_TOPK):
            if not k & d:
                _exchange(c, k, k + d)
        d //= 2
    return c


def _merge_sublanes(v, sort_last):
    for shift in (4, 2, 1):
        other = [pltpu.roll(x, shift, 0) for x in v]
        c = [jnp.maximum(v[k], other[PEER_TOPK - 1 - k]) for k in range(PEER_TOPK)]
        v = c if (shift == 1 and not sort_last) else _bitonic_sort_desc(c)
    return v


def _top16_sorted(s):
    v = [s[k * SUBLANES:(k + 1) * SUBLANES] for k in range(s.shape[0] // SUBLANES)]
    for i, j in SORT16_PAIRS:
        _exchange(v, i, j)
    return _merge_sublanes(v, True)


def _by_sublane(rows, sub8):
    out = rows[SUBLANES - 1]
    for b in range(SUBLANES - 2, -1, -1):
        out = jnp.where(sub8 == b, rows[b], out)
    return out


def _peer_gate_kernel(x_ref, g_ref, wq_ref, k1_ref, k2_ref,
                      xnt_ref, s1_ref, a_ref, s2_ref, bn_ref, tau_ref):
    xn = _rms(x_ref[...], g_ref[...])
    xnt_ref[...] = xn.T.astype(BF16)
    qp = jnp.dot(xn.astype(BF16), wq_ref[...], preferred_element_type=F32).astype(BF16)
    sub8 = lax.broadcasted_iota(jnp.int32, (8, GATE_TB), 0)
    nt = (((1,), (1,)), ((), ()))
    for h in range(PEER_HEADS):
        c0 = h * 2 * PEER_HALF
        s1 = lax.dot_general(k1_ref[...], qp[:, c0:c0 + PEER_HALF], nt, preferred_element_type=F32)
        s2 = lax.dot_general(k2_ref[...], qp[:, c0 + PEER_HALF:c0 + 2 * PEER_HALF], nt,
                             preferred_element_type=F32)
        t1 = _top16_sorted(s1)
        t2 = _top16_sorted(s2)
        t2_lo = _by_sublane(t2[:SUBLANES], sub8)
        t2_hi = _by_sublane(t2[SUBLANES:], sub8)
        lo = []
        for a in range(PEER_TOPK):
            nvalid = min(SUBLANES, PAIRS_PER_RANK[a])
            c = t1[a] + t2_lo
            lo.append(c if nvalid == SUBLANES else jnp.where(sub8 < nvalid, c, -jnp.inf))
        hi = t1[0] + t2_hi
        ins = [jnp.maximum(lo[0], hi)]
        ins += [jnp.maximum(lo[k], jnp.minimum(lo[k - 1], hi)) for k in range(1, PEER_TOPK)]
        best = _merge_sublanes(ins, False)
        tau = functools.reduce(jnp.minimum, best)
        e2_lo = jnp.exp(t2_lo - t2[0])
        zacc = jnp.where(hi >= tau, jnp.exp(t2_hi - t2[0]), 0.0)
        for a in range(PEER_TOPK):
            zacc = zacc + jnp.where(lo[a] >= tau, jnp.exp(t1[a] - t1[0]) * e2_lo, 0.0)
        z = jnp.sum(zacc, axis=0, keepdims=True)
        s1_ref[h] = s1
        s2_ref[h] = s2
        a_ref[h] = jnp.exp(s1 - t1[0][0:1])
        bn_ref[h] = jnp.exp(s2 - t2[0][0:1]) / z
        tau_ref[h] = tau[0:1]


def _peer_gate(x1, g, wq, k1, k2):
    tb = GATE_TB
    big = pl.BlockSpec((PEER_HEADS, N_KEYS, tb), lambda i: (0, 0, i))
    big_shape = jax.ShapeDtypeStruct((PEER_HEADS, N_KEYS, N_PAD), F32)
    return pl.pallas_call(
        _peer_gate_kernel,
        grid=(N_PAD // tb,),
        in_specs=[
            pl.BlockSpec((tb, D_MODEL), lambda i: (i, 0)),
            pl.BlockSpec((1, D_MODEL), lambda i: (0, 0)),
            _const_spec((D_MODEL, PEER_HEADS * 2 * PEER_HALF)),
            _const_spec((N_KEYS, PEER_HALF)),
            _const_spec((N_KEYS, PEER_HALF)),
        ],
        out_specs=[
            pl.BlockSpec((D_MODEL, tb), lambda i: (0, i)),
            big, big, big, big,
            pl.BlockSpec((PEER_HEADS, 1, tb), lambda i: (0, 0, i)),
        ],
        out_shape=[
            jax.ShapeDtypeStruct((D_MODEL, N_PAD), BF16),
            big_shape, big_shape, big_shape, big_shape,
            jax.ShapeDtypeStruct((PEER_HEADS, 1, N_PAD), F32),
        ],
        compiler_params=_params(("parallel",)),
        name="peer_gate",
    )(x1, g, wq, k1, k2)


DENSE_TB = 768
DENSE_EB = 1024
DENSE_EB_F32 = 512
N_DENSE_TOKEN_BLOCKS = N_PAD // DENSE_TB


def _peer_dense_kernel(from_f32, xnt_ref, s1_ref, a_ref, s2_ref, bn_ref, tau_ref, u_ref, v_ref, *rest):
    if from_f32:
        o_ref, ub_ref, vtb_ref, st_ref, wt_ref = rest
        u = u_ref[...].astype(BF16)
        v = v_ref[...].astype(BF16)
        ub_ref[...] = u
        vtb_ref[...] = v.T
    else:
        o_ref, st_ref, wt_ref = rest
    n_first = st_ref.shape[0] // N_KEYS

    @pl.when(pl.program_id(1) == 0)
    def _():
        o_ref[...] = jnp.zeros_like(o_ref)

    st_ref[...] = jnp.dot(u if from_f32 else u_ref[...], xnt_ref[...], preferred_element_type=F32)

    for ii in range(n_first):
        rows = slice(ii * N_KEYS, (ii + 1) * N_KEYS)
        for c in range(DENSE_TB // LANE):
            lanes = slice(c * LANE, (c + 1) * LANE)
            gate = jnp.zeros((N_KEYS, LANE), F32)
            for h in range(PEER_HEADS):
                s1_row = s1_ref[h, ii:ii + 1, lanes]
                a_row = a_ref[h, ii:ii + 1, lanes]
                hit = (s1_row + s2_ref[h, :, lanes]) >= tau_ref[h, :, lanes]
                gate = gate + jnp.where(hit, a_row * bn_ref[h, :, lanes], 0.0)
            act = jax.nn.gelu(st_ref[rows, lanes])
            wt_ref[rows, lanes] = (gate * act).astype(BF16)

    if from_f32:
        o_ref[...] += lax.dot_general(v, wt_ref[...], (((0,), (0,)), ((), ())), preferred_element_type=F32)
    else:
        o_ref[...] += jnp.dot(v_ref[...], wt_ref[...], preferred_element_type=F32)


def _peer_dense(xnt, s1t, at, s2t, bnt, tau, u_f32, v_f32):
    tb = DENSE_TB
    once = pl.Buffered(1)

    def token_specs(t_of):
        per_tok = lambda shape: pl.BlockSpec(shape, lambda t, e: (0,) * (len(shape) - 1) + (t_of(t),),
                                             pipeline_mode=once)
        return [per_tok((D_MODEL, tb)), per_tok((PEER_HEADS, N_KEYS, tb)), per_tok((PEER_HEADS, N_KEYS, tb)),
                per_tok((PEER_HEADS, 1, tb))]

    eb = DENSE_EB_F32
    ni = eb // N_KEYS
    n_e = N_EXPERTS // eb
    halves = DENSE_EB // eb
    first_keys = lambda a: a[:, :, :tb].reshape(PEER_HEADS, n_e, ni, tb)
    by_key0 = pl.BlockSpec((PEER_HEADS, None, ni, tb), lambda t, e: (0, e, 0, 0))
    xnt_s, s2_s, bn_s, tau_s = token_specs(lambda t: 0)
    out0, u_bf, vt_bf = pl.pallas_call(
        functools.partial(_peer_dense_kernel, True),
        grid=(1, n_e),
        in_specs=[xnt_s, by_key0, by_key0, s2_s, bn_s, tau_s,
                  pl.BlockSpec((eb, D_MODEL), lambda t, e: (e, 0)),
                  pl.BlockSpec((eb, D_MODEL), lambda t, e: (e, 0))],
        out_specs=[
            pl.BlockSpec((D_MODEL, tb), lambda t, e: (0, 0), pipeline_mode=once),
            pl.BlockSpec((eb, D_MODEL), lambda t, e: (e, 0)),
            pl.BlockSpec((None, D_MODEL, eb), lambda t, e: (e // halves, 0, e % halves)),
        ],
        out_shape=[
            jax.ShapeDtypeStruct((D_MODEL, tb), F32),
            jax.ShapeDtypeStruct((N_EXPERTS, D_MODEL), BF16),
            jax.ShapeDtypeStruct((N_EXPERTS // DENSE_EB, D_MODEL, DENSE_EB), BF16),
        ],
        scratch_shapes=[pltpu.VMEM((eb, tb), F32), pltpu.VMEM((eb, tb), BF16)],
        compiler_params=_params(("arbitrary", "arbitrary")),
        name="peer_dense_first",
    )(xnt, first_keys(s1t), first_keys(at), s2t, bnt, tau, u_f32, v_f32)

    eb = DENSE_EB
    by_key = pl.BlockSpec((PEER_HEADS, eb // N_KEYS, tb), lambda t, e: (0, e, t + 1))
    xnt_s, s2_s, bn_s, tau_s = token_specs(lambda t: t + 1)
    out_rest = pl.pallas_call(
        functools.partial(_peer_dense_kernel, False),
        grid=(N_DENSE_TOKEN_BLOCKS - 1, N_EXPERTS // eb),
        in_specs=[xnt_s, by_key, by_key, s2_s, bn_s, tau_s,
                  pl.BlockSpec((eb, D_MODEL), lambda t, e: (e, 0)),
                  pl.BlockSpec((None, D_MODEL, eb), lambda t, e: (e, 0, 0))],
        out_specs=pl.BlockSpec((D_MODEL, tb), lambda t, e: (0, t)),
        out_shape=jax.ShapeDtypeStruct((D_MODEL, N_PAD - tb), F32),
        scratch_shapes=[pltpu.VMEM((eb, tb), F32), pltpu.VMEM((eb, tb), BF16)],
        compiler_params=_params(("parallel", "arbitrary")),
        name="peer_dense",
    )(xnt, s1t, at, s2t, bnt, tau, u_bf, vt_bf)
    return out0, out_rest


FINAL_TM = 256
N_PROMPT_FINAL_BLOCKS = N_PROMPT // FINAL_TM


FINAL_FIRST_BLOCKS = DENSE_TB // FINAL_TM


def _final_kernel(x_ref, pt0_ref, pt_ref, pp_ref, ptail_ref, wg_ref, wp_ref, g_ref, yp_ref, ys_ref):
    i = pl.program_id(0)
    peer_t = jnp.where(i < FINAL_FIRST_BLOCKS, pt0_ref[...], pt_ref[...])
    x2 = x_ref[...] + peer_t.T
    gate = _sigmoid(jnp.dot(x2.astype(BF16), wg_ref[...], preferred_element_type=F32))
    emb = jnp.dot(_prompt_or_tail(pp_ref, ptail_ref).astype(BF16), wp_ref[...], preferred_element_type=F32)
    y = _rms(x2 + gate * emb, g_ref[...])

    @pl.when(i < N_PROMPT_FINAL_BLOCKS)
    def _():
        yp_ref[...] = y

    @pl.when(i == N_PROMPT_FINAL_BLOCKS)
    def _():
        ys_ref[...] = y[:DEC_BATCH]


def _final(x1, peer_t0, peer_t, p_prompt, p_tail, wg, wp, g):
    tm = FINAL_TM
    assert N_PAD // tm == N_PROMPT_FINAL_BLOCKS + 1 and DENSE_TB % tm == 0 and tm == TAIL_ROWS
    nf = FINAL_FIRST_BLOCKS
    return pl.pallas_call(
        _final_kernel,
        grid=(N_PAD // tm,),
        in_specs=[
            pl.BlockSpec((tm, D_MODEL), lambda i: (i, 0)),
            pl.BlockSpec((D_MODEL, tm), lambda i: (0, jnp.minimum(i, nf - 1))),
            pl.BlockSpec((D_MODEL, tm), lambda i: (0, jnp.maximum(i - nf, 0))),
            *_prompt_or_tail_specs(PLE_DIM),
            _const_spec((D_MODEL, D_MODEL)),
            _const_spec((PLE_DIM, D_MODEL)),
            pl.BlockSpec((1, D_MODEL), lambda i: (0, 0)),
        ],
        out_specs=[
            pl.BlockSpec((tm, D_MODEL), lambda i: (jnp.minimum(i, N_PROMPT_FINAL_BLOCKS - 1), 0)),
            pl.BlockSpec((DEC_BATCH, D_MODEL), lambda i: (0, 0)),
        ],
        out_shape=[
            jax.ShapeDtypeStruct((N_PROMPT, D_MODEL), F32),
            jax.ShapeDtypeStruct((DEC_BATCH, D_MODEL), F32),
        ],
        compiler_params=_params(("arbitrary",)),
        name="final",
    )(x1, peer_t0, peer_t, p_prompt, p_tail, wg, wp, g)


KV_ROWS = 512


def _kv_store(k, v, o_ref):
    for h in range(HEADS_PER_GROUP):
        cs = slice(h * HEAD_DIM, (h + 1) * HEAD_DIM)
        o_ref[:, 0, h, :] = k[:, cs]
        o_ref[:, 1, h, :] = v[:, cs]


def _kv_prompt_kernel(k_ref, v_ref, o_ref):
    _kv_store(k_ref[...], v_ref[...], o_ref)


def _kv_prompt(proj, gi, rows):
    blk = min(rows, KV_ROWS)
    nblk = rows // blk
    first = SEQ // blk - nblk
    src = lambda col: pl.BlockSpec(
        (blk, GROUP_WIDTH), lambda b, r: (b * (SEQ // blk) + first + r, col + gi))
    return pl.pallas_call(
        _kv_prompt_kernel,
        grid=(BATCH, nblk),
        in_specs=[src(COL_K), src(COL_V)],
        out_specs=pl.BlockSpec((None, None, blk, 2, HEADS_PER_GROUP, HEAD_DIM),
                               lambda b, r: (0, b, r, 0, 0, 0)),
        out_shape=jax.ShapeDtypeStruct((1, BATCH, rows, 2, HEADS_PER_GROUP, HEAD_DIM), F32),
        compiler_params=_params(("parallel", "parallel")),
        name="kv_prompt",
    )(proj, proj)


def _kv_sample_kernel(k0_ref, v0_ref, k1_ref, v1_ref, k2_ref, v2_ref, o0_ref, o1_ref, o2_ref):
    _kv_store(k0_ref[...], v0_ref[...], o0_ref)
    _kv_store(k1_ref[...], v1_ref[...], o1_ref)
    _kv_store(k2_ref[...], v2_ref[...], o2_ref)


def _kv_sample(proj):
    row_blk = N_PROMPT // DEC_BATCH
    src = lambda col: pl.BlockSpec((DEC_BATCH, GROUP_WIDTH), lambda i: (row_blk, col))
    in_specs = []
    for gi in range(len(DIL_GROUPS)):
        in_specs += [src(COL_K + gi), src(COL_V + gi)]
    out_spec = pl.BlockSpec((None, DEC_BATCH, None, 2, HEADS_PER_GROUP, HEAD_DIM),
                            lambda i: (0, 0, 0, 0, 0, 0))
    out_shape = jax.ShapeDtypeStruct((1, DEC_BATCH, 1, 2, HEADS_PER_GROUP, HEAD_DIM), F32)
    return pl.pallas_call(
        _kv_sample_kernel,
        grid=(1,),
        in_specs=in_specs,
        out_specs=[out_spec] * 3,
        out_shape=[out_shape] * 3,
        compiler_params=_params(("arbitrary",)),
        name="kv_sample",
    )(*([proj] * 6))


def _rope_tables():
    pos = jnp.concatenate([
        jnp.tile(jnp.arange(SEQ, dtype=jnp.int32), BATCH),
        jnp.full((DEC_BATCH,), PAST_LEN, jnp.int32),
        jnp.zeros((N_PAD - N_TOK,), jnp.int32)])
    inv = ROPE_THETA ** (-jnp.arange(ROT_HALF, dtype=F32) / ROT_HALF)
    ang = pos.astype(F32)[:, None] * inv[None, :]
    cos, sin = jnp.cos(ang), jnp.sin(ang)
    n = pos.shape[0]
    zeros = lambda w: jnp.zeros((n, w), F32)
    c = jnp.concatenate([cos, cos, jnp.ones((n, HEAD_DIM - 2 * ROT_HALF), F32)], axis=1)
    s_hi = jnp.concatenate([-sin, zeros(HEAD_DIM - ROT_HALF)], axis=1)
    s_lo = jnp.concatenate([zeros(ROT_HALF), sin, zeros(HEAD_DIM - 2 * ROT_HALF)], axis=1)
    return c, s_hi, s_lo


def kernel(x_prompt, x_sample, cache_kv_w128, cache_kv_w512, cache_kv_w2048, p_prompt, p_sample, g_mix, w_in, sgu_ln_g, sgu_ln_b, w_s, b_s, w_a_out, w_b_out, w_o, g_ffn, peer_w_q, peer_sub_k1, peer_sub_k2, peer_u, peer_v, w_ple, w_ple_gate, g_final):
    assert x_prompt.shape == (BATCH, SEQ, D_MODEL) and x_sample.shape == (DEC_BATCH, 1, D_MODEL)
    assert w_in.shape == (1, D_MODEL, IN_COLS)
    pad = N_PAD - N_TOK
    tail = lambda a: jnp.pad(a.reshape(DEC_BATCH, -1), ((0, pad), (0, 0)))
    x_p, x_t = x_prompt.reshape(N_PROMPT, D_MODEL), tail(x_sample)
    p_p, p_t = p_prompt.reshape(N_PROMPT, PLE_DIM), tail(p_sample)

    c, s_hi, s_lo = _rope_tables()
    proj = _proj(x_p, x_t, g_mix, w_in[0].astype(BF16), c, s_hi, s_lo)

    bsb = jnp.broadcast_to(b_s[0][:, :, None], (A_GROUPS, CHUNK, CHUNK))
    wd = jnp.repeat(w_s[0, :, 0, 0], CHUNK)[None]
    b0 = jnp.repeat(b_s[0, :, 0], CHUNK)[None]
    amix, vn_s = _sgu(proj, sgu_ln_g, sgu_ln_b, w_s[0], bsb, wd, b0)

    bmix_p = _attn_prompt(proj, [_attn_strided(proj, gi) for gi in STRIDED_GROUPS])
    cq, ck, cv = COL_Q * PROJ_TN, COL_K * PROJ_TN, COL_V * PROJ_TN
    srows = proj[N_PROMPT:N_TOK]
    per_head = lambda a: a.reshape(DEC_BATCH, len(DIL_GROUPS), HEADS_PER_GROUP, HEAD_DIM)
    caches = []
    for cache, (window, dil) in zip((cache_kv_w128, cache_kv_w512, cache_kv_w2048), DIL_GROUPS):
        l_buf = cache.shape[2]
        assert l_buf == N_KEYS * dil
        caches.append(cache.reshape(DEC_BATCH, N_KEYS, dil, 2, HEADS_PER_GROUP, HEAD_DIM))
    bmix_s = _attn_sample(per_head(srows[:, cq:ck]), per_head(srows[:, ck:cv]), per_head(srows[:, cv:cv + B_WIDTH]), caches)
    x1 = _merge(x_p, x_t, amix, bmix_p, tail(bmix_s), proj, w_a_out[0].astype(BF16), w_b_out[0].astype(BF16), w_o[0].astype(BF16))

    xnt, s1t, at, s2t, bnt, tau = _peer_gate(x1, g_ffn, peer_w_q[0].astype(BF16),
                                             peer_sub_k1[0].astype(BF16), peer_sub_k2[0].astype(BF16))
    peer_t0, peer_t = _peer_dense(xnt, s1t, at, s2t, bnt, tau, peer_u[0], peer_v[0])

    y_p, y_s = _final(x1, peer_t0, peer_t, p_p, p_t, w_ple_gate[0].astype(BF16), w_ple[0].astype(BF16), g_final[None])

    y_prompt = y_p.reshape(BATCH, SEQ, D_MODEL)
    y_sample = y_s.reshape(DEC_BATCH, 1, D_MODEL)
    kv_p = [_kv_prompt(proj, gi, min(window, SEQ)) for gi, (window, _) in enumerate(DIL_GROUPS)]
    kv_s = _kv_sample(proj)
    sgu_v_sample = vn_s.reshape(1, DEC_BATCH, 1, A_WIDTH)
    return (y_prompt, y_sample, kv_p[0], kv_p[1], kv_p[2], kv_s[0], kv_s[1], kv_s[2], sgu_v_sample)
```

```python
import functools
import math

import jax
import jax.numpy as jnp
from jax import lax
from jax.experimental import pallas as pl
from jax.experimental.pallas import tpu as pltpu

F32 = jnp.float32
BF16 = jnp.bfloat16

D_MODEL = 2048
BATCH = 4
SEQ = 2048
DEC_BATCH = 128
PAST_LEN = 2048
EPS = 1e-6
CHUNK = 128
A_GROUPS = 8
A_WIDTH = 1024
HEAD_DIM = 128
HEADS_PER_GROUP = 4
DIL_GROUPS = ((128, 1), (512, 4), (2048, 16))
GROUP_WIDTH = HEADS_PER_GROUP * HEAD_DIM
B_WIDTH = 3 * GROUP_WIDTH
ATTN_SCALE = HEAD_DIM ** -0.5
ROPE_THETA = 500000.0
ROT_HALF = HEAD_DIM // 8
IN_COLS = 2 * A_WIDTH + 3 * B_WIDTH + 2 * D_MODEL
N_KEYS = 128
N_EXPERTS = N_KEYS * N_KEYS
PEER_HEADS = 8
PEER_HALF = 128
PEER_TOPK = 16
PLE_DIM = 256

N_PROMPT = BATCH * SEQ
N_TOK = N_PROMPT + DEC_BATCH
N_PAD = 8448
LANE = 128

PROJ_TN = 512
COL_GATE = 2 * A_WIDTH // PROJ_TN
COL_Q = COL_GATE + 2 * D_MODEL // PROJ_TN
COL_K = COL_Q + B_WIDTH // PROJ_TN
COL_V = COL_K + B_WIDTH // PROJ_TN
N_COL_BLOCKS = IN_COLS // PROJ_TN

VMEM_LIMIT = 56 * 1024 * 1024


def _params(sem, vmem=VMEM_LIMIT):
    return pltpu.CompilerParams(dimension_semantics=sem, vmem_limit_bytes=vmem)


def _const_spec(shape):
    nd = len(shape)
    return pl.BlockSpec(shape, lambda *_: (0,) * nd, pipeline_mode=pl.Buffered(1))


def _rms(x, g):
    r = lax.rsqrt(jnp.mean(x * x, axis=-1, keepdims=True) + EPS)
    return (x * r) * g


def _sigmoid(x):
    return 1.0 / (1.0 + jnp.exp(-x))


PROJ_TM = 1024
PROJ_CHUNK = 256


PROJ_STEP_COLS = 2 * PROJ_TN
PROJ_COLS = -(-IN_COLS // PROJ_STEP_COLS) * PROJ_STEP_COLS


def _proj_kernel(cast_w, x_ref, g_ref, wa_ref, wb_ref, c_ref, s1_ref, s2_ref, o_ref, *rest):
    j = 2 * pl.program_id(1)
    if cast_w:
        w_out, h_ref = rest
        w_out[:, :PROJ_TN] = wa_ref[...].astype(BF16)
        w_out[:, PROJ_TN:] = wb_ref[...].astype(BF16)
        w_cols = lambda half, c0: w_out[:, half * PROJ_TN + c0:half * PROJ_TN + c0 + PROJ_CHUNK]
    else:
        h_ref, = rest
        w_cols = lambda half, c0: (wa_ref, wb_ref)[half][:, c0:c0 + PROJ_CHUNK]

    @pl.when(pl.program_id(1) == 0)
    def _():
        h_ref[...] = _rms(x_ref[...], g_ref[...]).astype(BF16)

    def rope(a):
        c = c_ref[...]
        s1 = s1_ref[...]
        s2 = s2_ref[...]
        heads = []
        for hh in range(a.shape[1] // HEAD_DIM):
            ah = a[:, hh * HEAD_DIM:(hh + 1) * HEAD_DIM]
            heads.append(ah * c + pltpu.roll(ah, HEAD_DIM - ROT_HALF, 1) * s1 + pltpu.roll(ah, ROT_HALF, 1) * s2)
        return jnp.concatenate(heads, axis=1)

    def project(epilogue):
        def branch():
            for half in range(2):
                for c0 in range(0, PROJ_TN, PROJ_CHUNK):
                    acc = jnp.dot(h_ref[...], w_cols(half, c0), preferred_element_type=F32)
                    o0 = half * PROJ_TN + c0
                    o_ref[:, o0:o0 + PROJ_CHUNK] = epilogue(acc)
        return branch

    pl.when(j < COL_GATE)(project(jax.nn.gelu))
    pl.when((j >= COL_GATE) & (j < COL_Q))(project(_sigmoid))
    pl.when((j >= COL_Q) & (j < COL_V))(project(rope))
    pl.when(j >= COL_V)(project(lambda a: a))


def _w_in_col_block(j):
    n_qkv = 3 * B_WIDTH // PROJ_TN
    n_gate = 2 * D_MODEL // PROJ_TN
    j = jnp.minimum(j, N_COL_BLOCKS - 1)
    return jnp.where(j < COL_GATE, j, jnp.where(j < COL_Q, j + n_qkv, j - n_gate))


def _proj_rows(x_rows, g, w, c, s1, s2, tm, first_blk, filled=None):
    n_blk = x_rows.shape[0] // tm
    cast_w = w.dtype == F32
    assert not (cast_w and (filled is not None or n_blk != 1))
    stored = lambda half: pl.BlockSpec((D_MODEL, PROJ_TN), lambda i, j: (0, 2 * j + half))
    if cast_w:
        w_half = lambda half: pl.BlockSpec((D_MODEL, PROJ_TN), lambda i, j: (0, _w_in_col_block(2 * j + half)))
    else:
        w_half = stored
    table = pl.BlockSpec((tm, HEAD_DIM), lambda i, j: (i + first_blk, 0))
    in_specs = [
        pl.BlockSpec((tm, D_MODEL), lambda i, j: (i, 0)),
        pl.BlockSpec((1, D_MODEL), lambda i, j: (0, 0)),
        w_half(0),
        w_half(1),
        table, table, table,
    ]
    args = [x_rows, g, w, w, c, s1, s2]
    kernel_fn, aliases = functools.partial(_proj_kernel, cast_w), {}
    if filled is not None:
        in_specs.append(pl.BlockSpec(memory_space=pl.ANY))
        args.append(filled)
        aliases = {len(args) - 1: 0}
        kernel_fn = lambda *refs: _proj_kernel(False, *refs[:7], *refs[8:])
    out_specs = [pl.BlockSpec((tm, PROJ_STEP_COLS), lambda i, j: (i + first_blk, j))]
    out_shape = [jax.ShapeDtypeStruct((N_PAD, PROJ_COLS), F32)]
    if cast_w:
        out_specs.append(pl.BlockSpec((D_MODEL, PROJ_STEP_COLS), lambda i, j: (0, j)))
        out_shape.append(jax.ShapeDtypeStruct((D_MODEL, PROJ_COLS), BF16))
    return pl.pallas_call(
        kernel_fn,
        grid=(n_blk, PROJ_COLS // PROJ_STEP_COLS),
        in_specs=in_specs,
        out_specs=out_specs,
        out_shape=out_shape,
        scratch_shapes=[pltpu.VMEM((tm, D_MODEL), BF16)],
        input_output_aliases=aliases,
        compiler_params=_params(("parallel", "arbitrary")),
        name="proj",
    )(*args)


def _proj(x_prompt, x_tail, g, w_f32, c, s1, s2):
    tail = x_tail.shape[0]
    assert N_PROMPT % PROJ_TM == 0 and N_PROMPT % tail == 0 and N_PROMPT + tail == N_PAD
    proj, w_bf = _proj_rows(x_tail, g, w_f32, c, s1, s2, tail, N_PROMPT // tail)
    return _proj_rows(x_prompt, g, w_bf, c, s1, s2, PROJ_TM, 0, filled=proj)[0]


N_PROMPT_CHUNKS = N_PROMPT // CHUNK


def _sgu_kernel(u_ref, gv_ref, lng_ref, lnb_ref, ws_ref, bsb_ref, wd_ref, b0_ref, o_ref, vn_ref):
    i = pl.program_id(0)
    gv = gv_ref[...]
    mu = jnp.mean(gv, axis=-1, keepdims=True)
    var = jnp.mean(jnp.square(gv - mu), axis=-1, keepdims=True)
    vn = ((gv - mu) * lax.rsqrt(var + EPS)) * lng_ref[...] + lnb_ref[...]

    @pl.when(i < N_PROMPT_CHUNKS)
    def _():
        row = lax.broadcasted_iota(jnp.int32, (CHUNK, CHUNK), 0)
        col = lax.broadcasted_iota(jnp.int32, (CHUNK, CHUNK), 1)
        causal = col <= row
        for g in range(A_GROUPS):
            cs = slice(g * CHUNK, (g + 1) * CHUNK)
            w = jnp.where(causal, ws_ref[g], 0.0).astype(BF16)
            mixed = jnp.dot(w, vn[:, cs].astype(BF16), preferred_element_type=F32) + bsb_ref[g]
            o_ref[:, cs] = u_ref[:, cs] * mixed

    @pl.when(i >= N_PROMPT_CHUNKS)
    def _():
        o_ref[...] = u_ref[...] * (vn * wd_ref[...] + b0_ref[...])

    @pl.when(i == N_PROMPT_CHUNKS)
    def _():
        vn_ref[...] = vn


def _sgu(proj, lng, lnb, ws, bsb, wd, b0):
    return pl.pallas_call(
        _sgu_kernel,
        grid=(N_PAD // CHUNK,),
        in_specs=[
            pl.BlockSpec((CHUNK, A_WIDTH), lambda i: (i, 0)),
            pl.BlockSpec((CHUNK, A_WIDTH), lambda i: (i, 1)),
            pl.BlockSpec((1, A_WIDTH), lambda i: (0, 0)),
            pl.BlockSpec((1, A_WIDTH), lambda i: (0, 0)),
            pl.BlockSpec((A_GROUPS, CHUNK, CHUNK), lambda i: (0, 0, 0)),
            pl.BlockSpec((A_GROUPS, CHUNK, CHUNK), lambda i: (0, 0, 0)),
            pl.BlockSpec((1, A_WIDTH), lambda i: (0, 0)),
            pl.BlockSpec((1, A_WIDTH), lambda i: (0, 0)),
        ],
        out_specs=[
            pl.BlockSpec((CHUNK, A_WIDTH), lambda i: (i, 0)),
            pl.BlockSpec((DEC_BATCH, A_WIDTH), lambda i: (0, 0)),
        ],
        out_shape=[
            jax.ShapeDtypeStruct((N_PAD, A_WIDTH), F32),
            jax.ShapeDtypeStruct((DEC_BATCH, A_WIDTH), F32),
        ],
        compiler_params=_params(("arbitrary",)),
        name="sgu",
    )(proj, proj, lng, lnb, ws, bsb, wd, b0)


Q_BLOCK = 128


def _softmax_pv(q, k, v, valid):
    s = lax.dot_general(q, k, (((1,), (1,)), ((), ())), preferred_element_type=F32) * ATTN_SCALE
    s = jnp.where(valid, s, -jnp.inf)
    m = jnp.max(s, axis=-1, keepdims=True)
    e = jnp.exp(s - m)
    den = jnp.sum(e, axis=-1, keepdims=True)
    o = jnp.dot((e / den).astype(BF16), v, preferred_element_type=F32)
    return o, m + jnp.log(den)


STRIDED_GROUPS = tuple(gi for gi, (_, d) in enumerate(DIL_GROUPS) if d > 1)
assert STRIDED_GROUPS == (1, 2) and all(w // d == Q_BLOCK for w, d in DIL_GROUPS)
STRIDED_UNROLL = 4


def _attn_strided_kernel(dil, q_ref, k_ref, v_ref, o_ref, l_ref):
    n_blk = SEQ // dil // Q_BLOCK
    n_keys = Q_BLOCK * min(n_blk, 2)
    dist0 = (lax.broadcasted_iota(jnp.int32, (Q_BLOCK, n_keys), 0)
             - lax.broadcasted_iota(jnp.int32, (Q_BLOCK, n_keys), 1))

    def tiles(it, carry):
        for u in range(STRIDED_UNROLL):
            tile = it * STRIDED_UNROLL + u
            r, n = tile // n_blk, tile % n_blk
            if n_blk == STRIDED_UNROLL:
                r, n = it, u
            first_key_blk = jnp.maximum(n - 1, 0)
            q_rows = pl.ds(r + dil * Q_BLOCK * n, Q_BLOCK, stride=dil)
            k_rows = pl.ds(r + dil * Q_BLOCK * first_key_blk, n_keys, stride=dil)
            dist = dist0 + Q_BLOCK * (n - first_key_blk)
            valid = jnp.where(dist >= 0, 1.0, 0.0) * jnp.where(dist <= Q_BLOCK, 1.0, 0.0) > 0.5
            o, lse = _softmax_pv(q_ref[q_rows, :].astype(BF16), k_ref[k_rows, :].astype(BF16),
                                 v_ref[k_rows, :].astype(BF16), valid)
            o_ref[q_rows, :] = o
            l_ref[q_rows, :] = jnp.broadcast_to(lse, (Q_BLOCK, HEAD_DIM))
        return carry

    lax.fori_loop(0, dil * n_blk // STRIDED_UNROLL, tiles, 0)


def _attn_strided(proj, gi):
    dil = DIL_GROUPS[gi][1]
    assert (SEQ // dil) % Q_BLOCK == 0 and (dil * (SEQ // dil // Q_BLOCK)) % STRIDED_UNROLL == 0
    heads_per_blk = PROJ_TN // HEAD_DIM
    src = lambda col: pl.BlockSpec(
        (SEQ, HEAD_DIM), lambda b, h: (b, (col + gi) * heads_per_blk + h))
    out = pl.BlockSpec((SEQ, HEAD_DIM), lambda b, h: (b, h))
    shape = jax.ShapeDtypeStruct((N_PROMPT, GROUP_WIDTH), F32)
    return pl.pallas_call(
        functools.partial(_attn_strided_kernel, dil),
        grid=(BATCH, HEADS_PER_GROUP),
        in_specs=[src(COL_Q), src(COL_K), src(COL_V)],
        out_specs=[out, out],
        out_shape=[shape, shape],
        compiler_params=_params(("parallel", "parallel")),
        name="attn_strided",
    )(proj, proj, proj)


def _attn_prompt_kernel(q_ref, k_ref, v_ref, o1_ref, l1_ref, o2_ref, l2_ref, o_ref):
    window = DIL_GROUPS[0][0]
    t0 = pl.multiple_of(pl.program_id(1) * Q_BLOCK, Q_BLOCK)
    n_keys = window + Q_BLOCK
    start = pl.multiple_of(jnp.maximum(t0 - window, 0), Q_BLOCK)
    dist = ((t0 - start) + lax.broadcasted_iota(jnp.int32, (Q_BLOCK, n_keys), 0)
            - lax.broadcasted_iota(jnp.int32, (Q_BLOCK, n_keys), 1))
    valid = jnp.where(dist >= 0, 1.0, 0.0) * jnp.where(dist <= window, 1.0, 0.0) > 0.5
    for h in range(HEADS_PER_GROUP):
        cs = slice(h * HEAD_DIM, (h + 1) * HEAD_DIM)
        o0, l0 = _softmax_pv(q_ref[:, cs].astype(BF16), k_ref[pl.ds(start, n_keys), cs].astype(BF16),
                             v_ref[pl.ds(start, n_keys), cs].astype(BF16), valid)
        l1, l2 = l1_ref[:, cs], l2_ref[:, cs]
        top = jnp.maximum(jnp.maximum(l0, l1), l2)
        w0, w1, w2 = jnp.exp(l0 - top), jnp.exp(l1 - top), jnp.exp(l2 - top)
        o_ref[:, cs] = (o0 * w0 + o1_ref[:, cs] * w1 + o2_ref[:, cs] * w2) / (w0 + w1 + w2)


def _attn_prompt(proj, strided):
    n_qb = SEQ // Q_BLOCK
    tile = pl.BlockSpec((Q_BLOCK, GROUP_WIDTH), lambda b, qb: (b * n_qb + qb, 0))
    (o1, l1), (o2, l2) = strided
    return pl.pallas_call(
        _attn_prompt_kernel,
        grid=(BATCH, n_qb),
        in_specs=[
            pl.BlockSpec((Q_BLOCK, GROUP_WIDTH), lambda b, qb: (b * n_qb + qb, COL_Q)),
            pl.BlockSpec((SEQ, GROUP_WIDTH), lambda b, qb: (b, COL_K)),
            pl.BlockSpec((SEQ, GROUP_WIDTH), lambda b, qb: (b, COL_V)),
            tile, tile, tile, tile,
        ],
        out_specs=tile,
        out_shape=jax.ShapeDtypeStruct((N_PROMPT, GROUP_WIDTH), F32),
        compiler_params=_params(("parallel", "arbitrary")),
        name="attn_prompt",
    )(proj, proj, proj, o1, l1, o2, l2)


SAMPLE_NB = 4


def _attn_sample_kernel(q_ref, kn_ref, vn_ref, k0_ref, v0_ref, k1_ref, v1_ref, k2_ref, v2_ref, o_ref):
    caches = ((k0_ref, v0_ref), (k1_ref, v1_ref), (k2_ref, v2_ref))

    def seq(n, carry):
        outs = []
        lses = []
        for g, (kc_ref, vc_ref) in enumerate(caches):
            q = q_ref[n, g]
            kc = kc_ref[n]
            s = jnp.sum(kc * q[None], axis=-1, keepdims=True) * ATTN_SCALE
            sn = jnp.sum(kn_ref[n, g] * q, axis=-1, keepdims=True) * ATTN_SCALE
            m = jnp.maximum(jnp.max(s, axis=0), sn)
            e = jnp.exp(s - m[None])
            en = jnp.exp(sn - m)
            den = jnp.sum(e, axis=0) + en
            o = (jnp.sum(e * vc_ref[n], axis=0) + en * vn_ref[n, g]) / den
            outs.append(o)
            lses.append(m + jnp.log(den))
        mx = jnp.maximum(jnp.maximum(lses[0], lses[1]), lses[2])
        ws = [jnp.exp(l - mx) for l in lses]
        tot = ws[0] + ws[1] + ws[2]
        o_ref[n] = (ws[0] * outs[0] + ws[1] * outs[1] + ws[2] * outs[2]) / tot
        return carry

    lax.fori_loop(0, SAMPLE_NB, seq, 0)


def _attn_sample(qs, kns, vns, caches):
    nb = SAMPLE_NB
    small = pl.BlockSpec((nb, len(DIL_GROUPS), HEADS_PER_GROUP, HEAD_DIM), lambda n: (n, 0, 0, 0))
    in_specs = [small, small, small]
    args = [qs, kns, vns]
    for c in caches:
        for kv in (0, 1):
            in_specs.append(pl.BlockSpec(
                (nb, N_KEYS, None, None, HEADS_PER_GROUP, HEAD_DIM),
                functools.partial(lambda n, kv_: (n, 0, 0, kv_, 0, 0), kv_=kv)))
            args.append(c)
    return pl.pallas_call(
        _attn_sample_kernel,
        grid=(DEC_BATCH // nb,),
        in_specs=in_specs,
        out_specs=pl.BlockSpec((nb, HEADS_PER_GROUP, HEAD_DIM), lambda n: (n, 0, 0)),
        out_shape=jax.ShapeDtypeStruct((DEC_BATCH, HEADS_PER_GROUP, HEAD_DIM), F32),
        compiler_params=_params(("parallel",)),
        name="attn_sample",
    )(*args)


TAIL_ROWS = N_PAD - N_PROMPT
MERGE_TM = TAIL_ROWS
N_PROMPT_BLOCKS = N_PROMPT // TAIL_ROWS


def _prompt_or_tail_specs(width):
    return [pl.BlockSpec((TAIL_ROWS, width), lambda i: (jnp.minimum(i, N_PROMPT_BLOCKS - 1), 0)),
            pl.BlockSpec((TAIL_ROWS, width), lambda i: (0, 0))]


def _prompt_or_tail(prompt_ref, tail_ref):
    return jnp.where(pl.program_id(0) < N_PROMPT_BLOCKS, prompt_ref[...], tail_ref[...])


def _merge_kernel(xp_ref, xt_ref, a_ref, bp_ref, bt_ref, ga_ref, gb_ref, wa_ref, wb_ref, wo_ref, o_ref):
    pa = jnp.dot(a_ref[...].astype(BF16), wa_ref[...], preferred_element_type=F32)
    pb = jnp.dot(_prompt_or_tail(bp_ref, bt_ref).astype(BF16), wb_ref[...], preferred_element_type=F32)
    merged = ga_ref[...] * pa + gb_ref[...] * pb
    o_ref[...] = _prompt_or_tail(xp_ref, xt_ref) + jnp.dot(merged.astype(BF16), wo_ref[...],
                                                          preferred_element_type=F32)


def _merge(x_prompt, x_tail, amix, bmix_prompt, bmix_tail, proj, wa, wb, wo):
    tm = MERGE_TM
    gate_a_blk = COL_GATE * PROJ_TN // D_MODEL
    return pl.pallas_call(
        _merge_kernel,
        grid=(N_PAD // tm,),
        in_specs=[
            *_prompt_or_tail_specs(D_MODEL),
            pl.BlockSpec((tm, A_WIDTH), lambda i: (i, 0)),
            *_prompt_or_tail_specs(GROUP_WIDTH),
            pl.BlockSpec((tm, D_MODEL), lambda i: (i, gate_a_blk)),
            pl.BlockSpec((tm, D_MODEL), lambda i: (i, gate_a_blk + 1)),
            _const_spec((A_WIDTH, D_MODEL)),
            _const_spec((GROUP_WIDTH, D_MODEL)),
            _const_spec((D_MODEL, D_MODEL)),
        ],
        out_specs=pl.BlockSpec((tm, D_MODEL), lambda i: (i, 0)),
        out_shape=jax.ShapeDtypeStruct((N_PAD, D_MODEL), F32),
        compiler_params=_params(("parallel",)),
        name="merge",
    )(x_prompt, x_tail, amix, bmix_prompt, bmix_tail, proj, proj, wa, wb, wo)


GATE_TB = 128
PAIRS_PER_RANK = tuple(PEER_TOPK // (a + 1) for a in range(PEER_TOPK))


SUBLANES = 8


def _batcher_pairs(n):
    pairs = []

    def merge(lo, m, r):
        step = 2 * r
        if step < m:
            merge(lo, m, step)
            merge(lo + r, m, step)
            pairs.extend((i, i + r) for i in range(lo + r, lo + m - r, step))
        else:
            pairs.append((lo, lo + r))

    def sort(lo, m):
        if m > 1:
            sort(lo, m // 2)
            sort(lo + m // 2, m // 2)
            merge(lo, m, 1)

    sort(0, n)
    return tuple(pairs)


SORT16_PAIRS = _batcher_pairs(PEER_TOPK)


def _exchange(v, i, j):
    v[i], v[j] = jnp.maximum(v[i], v[j]), jnp.minimum(v[i], v[j])


def _bitonic_sort_desc(c):
    c = list(c)
    d = PEER_TOPK // 2
    while d:
        for k in range(PEER_TOPK):
            if not k & d:
                _exchange(c, k, k + d)
        d //= 2
    return c


def _merge_sublanes(v, sort_last):
    for shift in (4, 2, 1):
        other = [pltpu.roll(x, shift, 0) for x in v]
        c = [jnp.maximum(v[k], other[PEER_TOPK - 1 - k]) for k in range(PEER_TOPK)]
        v = c if (shift == 1 and not sort_last) else _bitonic_sort_desc(c)
    return v


def _top16_sorted(s):
    v = [s[k * SUBLANES:(k + 1) * SUBLANES] for k in range(s.shape[0] // SUBLANES)]
    for i, j in SORT16_PAIRS:
        _exchange(v, i, j)
    return _merge_sublanes(v, True)


def _by_sublane(rows, sub8):
    out = rows[SUBLANES - 1]
    for b in range(SUBLANES - 2, -1, -1):
        out = jnp.where(sub8 == b, rows[b], out)
    return out


def _peer_gate_kernel(x_ref, g_ref, wq_ref, k1_ref, k2_ref,
                      xnt_ref, s1_ref, a_ref, s2_ref, bn_ref, tau_ref):
    xn = _rms(x_ref[...], g_ref[...])
    xnt_ref[...] = xn.T.astype(BF16)
    qp = jnp.dot(xn.astype(BF16), wq_ref[...], preferred_element_type=F32).astype(BF16)
    sub8 = lax.broadcasted_iota(jnp.int32, (8, GATE_TB), 0)
    nt = (((1,), (1,)), ((), ()))
    for h in range(PEER_HEADS):
        c0 = h * 2 * PEER_HALF
        s1 = lax.dot_general(k1_ref[...], qp[:, c0:c0 + PEER_HALF], nt, preferred_element_type=F32)
        s2 = lax.dot_general(k2_ref[...], qp[:, c0 + PEER_HALF:c0 + 2 * PEER_HALF], nt,
                             preferred_element_type=F32)
        t1 = _top16_sorted(s1)
        t2 = _top16_sorted(s2)
        t2_lo = _by_sublane(t2[:SUBLANES], sub8)
        t2_hi = _by_sublane(t2[SUBLANES:], sub8)
        lo = []
        for a in range(PEER_TOPK):
            nvalid = min(SUBLANES, PAIRS_PER_RANK[a])
            c = t1[a] + t2_lo
            lo.append(c if nvalid == SUBLANES else jnp.where(sub8 < nvalid, c, -jnp.inf))
        hi = t1[0] + t2_hi
        ins = [jnp.maximum(lo[0], hi)]
        ins += [jnp.maximum(lo[k], jnp.minimum(lo[k - 1], hi)) for k in range(1, PEER_TOPK)]
        best = _merge_sublanes(ins, False)
        tau = functools.reduce(jnp.minimum, best)
        e2_lo = jnp.exp(t2_lo - t2[0])
        zacc = jnp.where(hi >= tau, jnp.exp(t2_hi - t2[0]), 0.0)
        for a in range(PEER_TOPK):
            zacc = zacc + jnp.where(lo[a] >= tau, jnp.exp(t1[a] - t1[0]) * e2_lo, 0.0)
        z = jnp.sum(zacc, axis=0, keepdims=True)
        s1_ref[h] = s1
        s2_ref[h] = s2
        a_ref[h] = jnp.exp(s1 - t1[0][0:1])
        bn_ref[h] = jnp.exp(s2 - t2[0][0:1]) / z
        tau_ref[h] = tau[0:1]


def _peer_gate(x1, g, wq, k1, k2):
    tb = GATE_TB
    big = pl.BlockSpec((PEER_HEADS, N_KEYS, tb), lambda i: (0, 0, i))
    big_shape = jax.ShapeDtypeStruct((PEER_HEADS, N_KEYS, N_PAD), F32)
    return pl.pallas_call(
        _peer_gate_kernel,
        grid=(N_PAD // tb,),
        in_specs=[
            pl.BlockSpec((tb, D_MODEL), lambda i: (i, 0)),
            pl.BlockSpec((1, D_MODEL), lambda i: (0, 0)),
            _const_spec((D_MODEL, PEER_HEADS * 2 * PEER_HALF)),
            _const_spec((N_KEYS, PEER_HALF)),
            _const_spec((N_KEYS, PEER_HALF)),
        ],
        out_specs=[
            pl.BlockSpec((D_MODEL, tb), lambda i: (0, i)),
            big, big, big, big,
            pl.BlockSpec((PEER_HEADS, 1, tb), lambda i: (0, 0, i)),
        ],
        out_shape=[
            jax.ShapeDtypeStruct((D_MODEL, N_PAD), BF16),
            big_shape, big_shape, big_shape, big_shape,
            jax.ShapeDtypeStruct((PEER_HEADS, 1, N_PAD), F32),
        ],
        compiler_params=_params(("parallel",)),
        name="peer_gate",
    )(x1, g, wq, k1, k2)


DENSE_TB = 768
DENSE_EB = 1024
DENSE_EB_F32 = 512
N_DENSE_TOKEN_BLOCKS = N_PAD // DENSE_TB


def _peer_dense_kernel(from_f32, xnt_ref, s1_ref, a_ref, s2_ref, bn_ref, tau_ref, u_ref, v_ref, *rest):
    if from_f32:
        o_ref, ub_ref, vtb_ref, st_ref, wt_ref = rest
        u = u_ref[...].astype(BF16)
        v = v_ref[...].astype(BF16)
        ub_ref[...] = u
        vtb_ref[...] = v.T
    else:
        o_ref, st_ref, wt_ref = rest
    n_first = st_ref.shape[0] // N_KEYS

    @pl.when(pl.program_id(1) == 0)
    def _():
        o_ref[...] = jnp.zeros_like(o_ref)

    st_ref[...] = jnp.dot(u if from_f32 else u_ref[...], xnt_ref[...], preferred_element_type=F32)

    for ii in range(n_first):
        rows = slice(ii * N_KEYS, (ii + 1) * N_KEYS)
        for c in range(DENSE_TB // LANE):
            lanes = slice(c * LANE, (c + 1) * LANE)
            gate = jnp.zeros((N_KEYS, LANE), F32)
            for h in range(PEER_HEADS):
                s1_row = s1_ref[h, ii:ii + 1, lanes]
                a_row = a_ref[h, ii:ii + 1, lanes]
                hit = (s1_row + s2_ref[h, :, lanes]) >= tau_ref[h, :, lanes]
                gate = gate + jnp.where(hit, a_row * bn_ref[h, :, lanes], 0.0)
            act = jax.nn.gelu(st_ref[rows, lanes])
            wt_ref[rows, lanes] = (gate * act).astype(BF16)

    if from_f32:
        o_ref[...] += lax.dot_general(v, wt_ref[...], (((0,), (0,)), ((), ())), preferred_element_type=F32)
    else:
        o_ref[...] += jnp.dot(v_ref[...], wt_ref[...], preferred_element_type=F32)


def _peer_dense(xnt, s1t, at, s2t, bnt, tau, u_f32, v_f32):
    tb = DENSE_TB
    once = pl.Buffered(1)

    def token_specs(t_of):
        per_tok = lambda shape: pl.BlockSpec(shape, lambda t, e: (0,) * (len(shape) - 1) + (t_of(t),),
                                             pipeline_mode=once)
        return [per_tok((D_MODEL, tb)), per_tok((PEER_HEADS, N_KEYS, tb)), per_tok((PEER_HEADS, N_KEYS, tb)),
                per_tok((PEER_HEADS, 1, tb))]

    eb = DENSE_EB_F32
    ni = eb // N_KEYS
    n_e = N_EXPERTS // eb
    halves = DENSE_EB // eb
    first_keys = lambda a: a[:, :, :tb].reshape(PEER_HEADS, n_e, ni, tb)
    by_key0 = pl.BlockSpec((PEER_HEADS, None, ni, tb), lambda t, e: (0, e, 0, 0))
    xnt_s, s2_s, bn_s, tau_s = token_specs(lambda t: 0)
    out0, u_bf, vt_bf = pl.pallas_call(
        functools.partial(_peer_dense_kernel, True),
        grid=(1, n_e),
        in_specs=[xnt_s, by_key0, by_key0, s2_s, bn_s, tau_s,
                  pl.BlockSpec((eb, D_MODEL), lambda t, e: (e, 0)),
                  pl.BlockSpec((eb, D_MODEL), lambda t, e: (e, 0))],
        out_specs=[
            pl.BlockSpec((D_MODEL, tb), lambda t, e: (0, 0), pipeline_mode=once),
            pl.BlockSpec((eb, D_MODEL), lambda t, e: (e, 0)),
            pl.BlockSpec((None, D_MODEL, eb), lambda t, e: (e // halves, 0, e % halves)),
        ],
        out_shape=[
            jax.ShapeDtypeStruct((D_MODEL, tb), F32),
            jax.ShapeDtypeStruct((N_EXPERTS, D_MODEL), BF16),
            jax.ShapeDtypeStruct((N_EXPERTS // DENSE_EB, D_MODEL, DENSE_EB), BF16),
        ],
        scratch_shapes=[pltpu.VMEM((eb, tb), F32), pltpu.VMEM((eb, tb), BF16)],
        compiler_params=_params(("arbitrary", "arbitrary")),
        name="peer_dense_first",
    )(xnt, first_keys(s1t), first_keys(at), s2t, bnt, tau, u_f32, v_f32)

    eb = DENSE_EB
    by_key = pl.BlockSpec((PEER_HEADS, eb // N_KEYS, tb), lambda t, e: (0, e, t + 1))
    xnt_s, s2_s, bn_s, tau_s = token_specs(lambda t: t + 1)
    out_rest = pl.pallas_call(
        functools.partial(_peer_dense_kernel, False),
        grid=(N_DENSE_TOKEN_BLOCKS - 1, N_EXPERTS // eb),
        in_specs=[xnt_s, by_key, by_key, s2_s, bn_s, tau_s,
                  pl.BlockSpec((eb, D_MODEL), lambda t, e: (e, 0)),
                  pl.BlockSpec((None, D_MODEL, eb), lambda t, e: (e, 0, 0))],
        out_specs=pl.BlockSpec((D_MODEL, tb), lambda t, e: (0, t)),
        out_shape=jax.ShapeDtypeStruct((D_MODEL, N_PAD - tb), F32),
        scratch_shapes=[pltpu.VMEM((eb, tb), F32), pltpu.VMEM((eb, tb), BF16)],
        compiler_params=_params(("parallel", "arbitrary")),
        name="peer_dense",
    )(xnt, s1t, at, s2t, bnt, tau, u_bf, vt_bf)
    return out0, out_rest


FINAL_TM = 256
N_PROMPT_FINAL_BLOCKS = N_PROMPT // FINAL_TM


FINAL_FIRST_BLOCKS = DENSE_TB // FINAL_TM


def _final_kernel(x_ref, pt0_ref, pt_ref, pp_ref, ptail_ref, wg_ref, wp_ref, g_ref, yp_ref, ys_ref):
    i = pl.program_id(0)
    peer_t = jnp.where(i < FINAL_FIRST_BLOCKS, pt0_ref[...], pt_ref[...])
    x2 = x_ref[...] + peer_t.T
    gate = _sigmoid(jnp.dot(x2.astype(BF16), wg_ref[...], preferred_element_type=F32))
    emb = jnp.dot(_prompt_or_tail(pp_ref, ptail_ref).astype(BF16), wp_ref[...], preferred_element_type=F32)
    y = _rms(x2 + gate * emb, g_ref[...])

    @pl.when(i < N_PROMPT_FINAL_BLOCKS)
    def _():
        yp_ref[...] = y

    @pl.when(i == N_PROMPT_FINAL_BLOCKS)
    def _():
        ys_ref[...] = y[:DEC_BATCH]


def _final(x1, peer_t0, peer_t, p_prompt, p_tail, wg, wp, g):
    tm = FINAL_TM
    assert N_PAD // tm == N_PROMPT_FINAL_BLOCKS + 1 and DENSE_TB % tm == 0 and tm == TAIL_ROWS
    nf = FINAL_FIRST_BLOCKS
    return pl.pallas_call(
        _final_kernel,
        grid=(N_PAD // tm,),
        in_specs=[
            pl.BlockSpec((tm, D_MODEL), lambda i: (i, 0)),
            pl.BlockSpec((D_MODEL, tm), lambda i: (0, jnp.minimum(i, nf - 1))),
            pl.BlockSpec((D_MODEL, tm), lambda i: (0, jnp.maximum(i - nf, 0))),
            *_prompt_or_tail_specs(PLE_DIM),
            _const_spec((D_MODEL, D_MODEL)),
            _const_spec((PLE_DIM, D_MODEL)),
            pl.BlockSpec((1, D_MODEL), lambda i: (0, 0)),
        ],
        out_specs=[
            pl.BlockSpec((tm, D_MODEL), lambda i: (jnp.minimum(i, N_PROMPT_FINAL_BLOCKS - 1), 0)),
            pl.BlockSpec((DEC_BATCH, D_MODEL), lambda i: (0, 0)),
        ],
        out_shape=[
            jax.ShapeDtypeStruct((N_PROMPT, D_MODEL), F32),
            jax.ShapeDtypeStruct((DEC_BATCH, D_MODEL), F32),
        ],
        compiler_params=_params(("arbitrary",)),
        name="final",
    )(x1, peer_t0, peer_t, p_prompt, p_tail, wg, wp, g)


KV_ROWS = 512


def _kv_store(k, v, o_ref):
    for h in range(HEADS_PER_GROUP):
        cs = slice(h * HEAD_DIM, (h + 1) * HEAD_DIM)
        o_ref[:, 0, h, :] = k[:, cs]
        o_ref[:, 1, h, :] = v[:, cs]


def _kv_prompt_kernel(k_ref, v_ref, o_ref):
    _kv_store(k_ref[...], v_ref[...], o_ref)


def _kv_prompt(proj, gi, rows):
    blk = min(rows, KV_ROWS)
    nblk = rows // blk
    first = SEQ // blk - nblk
    src = lambda col: pl.BlockSpec(
        (blk, GROUP_WIDTH), lambda b, r: (b * (SEQ // blk) + first + r, col + gi))
    return pl.pallas_call(
        _kv_prompt_kernel,
        grid=(BATCH, nblk),
        in_specs=[src(COL_K), src(COL_V)],
        out_specs=pl.BlockSpec((None, None, blk, 2, HEADS_PER_GROUP, HEAD_DIM),
                               lambda b, r: (0, b, r, 0, 0, 0)),
        out_shape=jax.ShapeDtypeStruct((1, BATCH, rows, 2, HEADS_PER_GROUP, HEAD_DIM), F32),
        compiler_params=_params(("parallel", "parallel")),
        name="kv_prompt",
    )(proj, proj)


def _kv_sample_kernel(k0_ref, v0_ref, k1_ref, v1_ref, k2_ref, v2_ref, o0_ref, o1_ref, o2_ref):
    _kv_store(k0_ref[...], v0_ref[...], o0_ref)
    _kv_store(k1_ref[...], v1_ref[...], o1_ref)
    _kv_store(k2_ref[...], v2_ref[...], o2_ref)


def _kv_sample(proj):
    row_blk = N_PROMPT // DEC_BATCH
    src = lambda col: pl.BlockSpec((DEC_BATCH, GROUP_WIDTH), lambda i: (row_blk, col))
    in_specs = []
    for gi in range(len(DIL_GROUPS)):
        in_specs += [src(COL_K + gi), src(COL_V + gi)]
    out_spec = pl.BlockSpec((None, DEC_BATCH, None, 2, HEADS_PER_GROUP, HEAD_DIM),
                            lambda i: (0, 0, 0, 0, 0, 0))
    out_shape = jax.ShapeDtypeStruct((1, DEC_BATCH, 1, 2, HEADS_PER_GROUP, HEAD_DIM), F32)
    return pl.pallas_call(
        _kv_sample_kernel,
        grid=(1,),
        in_specs=in_specs,
        out_specs=[out_spec] * 3,
        out_shape=[out_shape] * 3,
        compiler_params=_params(("arbitrary",)),
        name="kv_sample",
    )(*([proj] * 6))


def _rope_tables():
    pos = jnp.concatenate([
        jnp.tile(jnp.arange(SEQ, dtype=jnp.int32), BATCH),
        jnp.full((DEC_BATCH,), PAST_LEN, jnp.int32),
        jnp.zeros((N_PAD - N_TOK,), jnp.int32)])
    inv = ROPE_THETA ** (-jnp.arange(ROT_HALF, dtype=F32) / ROT_HALF)
    ang = pos.astype(F32)[:, None] * inv[None, :]
    cos, sin = jnp.cos(ang), jnp.sin(ang)
    n = pos.shape[0]
    zeros = lambda w: jnp.zeros((n, w), F32)
    c = jnp.concatenate([cos, cos, jnp.ones((n, HEAD_DIM - 2 * ROT_HALF), F32)], axis=1)
    s_hi = jnp.concatenate([-sin, zeros(HEAD_DIM - ROT_HALF)], axis=1)
    s_lo = jnp.concatenate([zeros(ROT_HALF), sin, zeros(HEAD_DIM - 2 * ROT_HALF)], axis=1)
    return c, s_hi, s_lo


def kernel(x_prompt, x_sample, cache_kv_w128, cache_kv_w512, cache_kv_w2048, p_prompt, p_sample, g_mix, w_in, sgu_ln_g, sgu_ln_b, w_s, b_s, w_a_out, w_b_out, w_o, g_ffn, peer_w_q, peer_sub_k1, peer_sub_k2, peer_u, peer_v, w_ple, w_ple_gate, g_final):
    assert x_prompt.shape == (BATCH, SEQ, D_MODEL) and x_sample.shape == (DEC_BATCH, 1, D_MODEL)
    assert w_in.shape == (1, D_MODEL, IN_COLS)
    pad = N_PAD - N_TOK
    tail = lambda a: jnp.pad(a.reshape(DEC_BATCH, -1), ((0, pad), (0, 0)))
    x_p, x_t = x_prompt.reshape(N_PROMPT, D_MODEL), tail(x_sample)
    p_p, p_t = p_prompt.reshape(N_PROMPT, PLE_DIM), tail(p_sample)

    c, s_hi, s_lo = _rope_tables()
    proj = _proj(x_p, x_t, g_mix, w_in[0], c, s_hi, s_lo)

    bsb = jnp.broadcast_to(b_s[0][:, :, None], (A_GROUPS, CHUNK, CHUNK))
    wd = jnp.repeat(w_s[0, :, 0, 0], CHUNK)[None]
    b0 = jnp.repeat(b_s[0, :, 0], CHUNK)[None]
    amix, vn_s = _sgu(proj, sgu_ln_g, sgu_ln_b, w_s[0], bsb, wd, b0)

    bmix_p = _attn_prompt(proj, [_attn_strided(proj, gi) for gi in STRIDED_GROUPS])
    cq, ck, cv = COL_Q * PROJ_TN, COL_K * PROJ_TN, COL_V * PROJ_TN
    srows = proj[N_PROMPT:N_TOK]
    per_head = lambda a: a.reshape(DEC_BATCH, len(DIL_GROUPS), HEADS_PER_GROUP, HEAD_DIM)
    caches = []
    for cache, (window, dil) in zip((cache_kv_w128, cache_kv_w512, cache_kv_w2048), DIL_GROUPS):
        l_buf = cache.shape[2]
        assert l_buf == N_KEYS * dil
        caches.append(cache.reshape(DEC_BATCH, N_KEYS, dil, 2, HEADS_PER_GROUP, HEAD_DIM))
    bmix_s = _attn_sample(per_head(srows[:, cq:ck]), per_head(srows[:, ck:cv]), per_head(srows[:, cv:cv + B_WIDTH]), caches)
    x1 = _merge(x_p, x_t, amix, bmix_p, tail(bmix_s), proj, w_a_out[0].astype(BF16), w_b_out[0].astype(BF16), w_o[0].astype(BF16))

    xnt, s1t, at, s2t, bnt, tau = _peer_gate(x1, g_ffn, peer_w_q[0].astype(BF16),
                                             peer_sub_k1[0].astype(BF16), peer_sub_k2[0].astype(BF16))
    peer_t0, peer_t = _peer_dense(xnt, s1t, at, s2t, bnt, tau, peer_u[0], peer_v[0])

    y_p, y_s = _final(x1, peer_t0, peer_t, p_p, p_t, w_ple_gate[0].astype(BF16), w_ple[0].astype(BF16), g_final[None])

    y_prompt = y_p.reshape(BATCH, SEQ, D_MODEL)
    y_sample = y_s.reshape(DEC_BATCH, 1, D_MODEL)
    kv_p = [_kv_prompt(proj, gi, min(window, SEQ)) for gi, (window, _) in enumerate(DIL_GROUPS)]
    kv_s = _kv_sample(proj)
    sgu_v_sample = vn_s.reshape(1, DEC_BATCH, 1, A_WIDTH)
    return (y_prompt, y_sample, kv_p[0], kv_p[1], kv_p[2], kv_s[0], kv_s[1], kv_s[2], sgu_v_sample)
```

```python
import functools
import math

import jax
import jax.numpy as jnp
from jax import lax
from jax.experimental import pallas as pl
from jax.experimental.pallas import tpu as pltpu

F32 = jnp.float32
BF16 = jnp.bfloat16

D_MODEL = 2048
BATCH = 4
SEQ = 2048
DEC_BATCH = 128
PAST_LEN = 2048
EPS = 1e-6
CHUNK = 128
A_GROUPS = 8
A_WIDTH = 1024
HEAD_DIM = 128
HEADS_PER_GROUP = 4
DIL_GROUPS = ((128, 1), (512, 4), (2048, 16))
GROUP_WIDTH = HEADS_PER_GROUP * HEAD_DIM
B_WIDTH = 3 * GROUP_WIDTH
ATTN_SCALE = HEAD_DIM ** -0.5
ROPE_THETA = 500000.0
ROT_HALF = HEAD_DIM // 8
IN_COLS = 2 * A_WIDTH + 3 * B_WIDTH + 2 * D_MODEL
N_KEYS = 128
N_EXPERTS = N_KEYS * N_KEYS
PEER_HEADS = 8
PEER_HALF = 128
PEER_TOPK = 16
PLE_DIM = 256

N_PROMPT = BATCH * SEQ
N_TOK = N_PROMPT + DEC_BATCH
N_PAD = 8448
LANE = 128

PROJ_TN = 512
COL_GATE = 2 * A_WIDTH // PROJ_TN
COL_Q = COL_GATE + 2 * D_MODEL // PROJ_TN
COL_K = COL_Q + B_WIDTH // PROJ_TN
COL_V = COL_K + B_WIDTH // PROJ_TN
N_COL_BLOCKS = IN_COLS // PROJ_TN

VMEM_LIMIT = 56 * 1024 * 1024


def _params(sem, vmem=VMEM_LIMIT):
    return pltpu.CompilerParams(dimension_semantics=sem, vmem_limit_bytes=vmem)


def _const_spec(shape):
    nd = len(shape)
    return pl.BlockSpec(shape, lambda *_: (0,) * nd, pipeline_mode=pl.Buffered(1))


def _rms(x, g):
    r = lax.rsqrt(jnp.mean(x * x, axis=-1, keepdims=True) + EPS)
    return (x * r) * g


def _sigmoid(x):
    return 1.0 / (1.0 + jnp.exp(-x))


PROJ_TM = 1024
PROJ_CHUNK = 256


PROJ_STEP_COLS = 2 * PROJ_TN
PROJ_COLS = -(-IN_COLS // PROJ_STEP_COLS) * PROJ_STEP_COLS


def _proj_kernel(cast_w, x_ref, g_ref, wa_ref, wb_ref, c_ref, s1_ref, s2_ref, o_ref, *rest):
    j = 2 * pl.program_id(1)
    if cast_w:
        w_out, h_ref = rest
        w_out[:, :PROJ_TN] = wa_ref[...].astype(BF16)
        w_out[:, PROJ_TN:] = wb_ref[...].astype(BF16)
        w_cols = lambda half, c0: w_out[:, half * PROJ_TN + c0:half * PROJ_TN + c0 + PROJ_CHUNK]
    else:
        h_ref, = rest
        w_cols = lambda half, c0: (wa_ref, wb_ref)[half][:, c0:c0 + PROJ_CHUNK]

    @pl.when(pl.program_id(1) == 0)
    def _():
        h_ref[...] = _rms(x_ref[...], g_ref[...]).astype(BF16)

    def rope(a):
        c = c_ref[...]
        s1 = s1_ref[...]
        s2 = s2_ref[...]
        heads = []
        for hh in range(a.shape[1] // HEAD_DIM):
            ah = a[:, hh * HEAD_DIM:(hh + 1) * HEAD_DIM]
            heads.append(ah * c + pltpu.roll(ah, HEAD_DIM - ROT_HALF, 1) * s1 + pltpu.roll(ah, ROT_HALF, 1) * s2)
        return jnp.concatenate(heads, axis=1)

    def project(epilogue):
        def branch():
            for half in range(2):
                for c0 in range(0, PROJ_TN, PROJ_CHUNK):
                    acc = jnp.dot(h_ref[...], w_cols(half, c0), preferred_element_type=F32)
                    o0 = half * PROJ_TN + c0
                    o_ref[:, o0:o0 + PROJ_CHUNK] = epilogue(acc)
        return branch

    pl.when(j < COL_GATE)(project(jax.nn.gelu))
    pl.when((j >= COL_GATE) & (j < COL_Q))(project(_sigmoid))
    pl.when((j >= COL_Q) & (j < COL_V))(project(rope))
    pl.when(j >= COL_V)(project(lambda a: a))


def _w_in_col_block(j):
    n_qkv = 3 * B_WIDTH // PROJ_TN
    n_gate = 2 * D_MODEL // PROJ_TN
    j = jnp.minimum(j, N_COL_BLOCKS - 1)
    return jnp.where(j < COL_GATE, j, jnp.where(j < COL_Q, j + n_qkv, j - n_gate))


def _proj_rows(x_rows, g, w, c, s1, s2, tm, first_blk, filled=None):
    n_blk = x_rows.shape[0] // tm
    cast_w = w.dtype == F32
    assert not (cast_w and (filled is not None or n_blk != 1))
    stored = lambda half: pl.BlockSpec((D_MODEL, PROJ_TN), lambda i, j: (0, 2 * j + half))
    if cast_w:
        w_half = lambda half: pl.BlockSpec((D_MODEL, PROJ_TN), lambda i, j: (0, _w_in_col_block(2 * j + half)))
    else:
        w_half = stored
    table = pl.BlockSpec((tm, HEAD_DIM), lambda i, j: (i + first_blk, 0))
    in_specs = [
        pl.BlockSpec((tm, D_MODEL), lambda i, j: (i, 0)),
        pl.BlockSpec((1, D_MODEL), lambda i, j: (0, 0)),
        w_half(0),
        w_half(1),
        table, table, table,
    ]
    args = [x_rows, g, w, w, c, s1, s2]
    kernel_fn, aliases = functools.partial(_proj_kernel, cast_w), {}
    if filled is not None:
        in_specs.append(pl.BlockSpec(memory_space=pl.ANY))
        args.append(filled)
        aliases = {len(args) - 1: 0}
        kernel_fn = lambda *refs: _proj_kernel(False, *refs[:7], *refs[8:])
    out_specs = [pl.BlockSpec((tm, PROJ_STEP_COLS), lambda i, j: (i + first_blk, j))]
    out_shape = [jax.ShapeDtypeStruct((N_PAD, PROJ_COLS), F32)]
    if cast_w:
        out_specs.append(pl.BlockSpec((D_MODEL, PROJ_STEP_COLS), lambda i, j: (0, j)))
        out_shape.append(jax.ShapeDtypeStruct((D_MODEL, PROJ_COLS), BF16))
    return pl.pallas_call(
        kernel_fn,
        grid=(n_blk, PROJ_COLS // PROJ_STEP_COLS),
        in_specs=in_specs,
        out_specs=out_specs,
        out_shape=out_shape,
        scratch_shapes=[pltpu.VMEM((tm, D_MODEL), BF16)],
        input_output_aliases=aliases,
        compiler_params=_params(("parallel", "arbitrary")),
        name="proj",
    )(*args)


def _proj(x_prompt, x_tail, g, w_f32, c, s1, s2):
    tail = x_tail.shape[0]
    assert N_PROMPT % PROJ_TM == 0 and N_PROMPT % tail == 0 and N_PROMPT + tail == N_PAD
    proj, w_bf = _proj_rows(x_tail, g, w_f32, c, s1, s2, tail, N_PROMPT // tail)
    return _proj_rows(x_prompt, g, w_bf, c, s1, s2, PROJ_TM, 0, filled=proj)[0]


N_PROMPT_CHUNKS = N_PROMPT // CHUNK


def _sgu_kernel(u_ref, gv_ref, lng_ref, lnb_ref, ws_ref, bsb_ref, wd_ref, b0_ref, o_ref, vn_ref):
    i = pl.program_id(0)
    gv = gv_ref[...]
    mu = jnp.mean(gv, axis=-1, keepdims=True)
    var = jnp.mean(jnp.square(gv - mu), axis=-1, keepdims=True)
    vn = ((gv - mu) * lax.rsqrt(var + EPS)) * lng_ref[...] + lnb_ref[...]

    @pl.when(i < N_PROMPT_CHUNKS)
    def _():
        row = lax.broadcasted_iota(jnp.int32, (CHUNK, CHUNK), 0)
        col = lax.broadcasted_iota(jnp.int32, (CHUNK, CHUNK), 1)
        causal = col <= row
        for g in range(A_GROUPS):
            cs = slice(g * CHUNK, (g + 1) * CHUNK)
            w = jnp.where(causal, ws_ref[g], 0.0).astype(BF16)
            mixed = jnp.dot(w, vn[:, cs].astype(BF16), preferred_element_type=F32) + bsb_ref[g]
            o_ref[:, cs] = u_ref[:, cs] * mixed

    @pl.when(i >= N_PROMPT_CHUNKS)
    def _():
        o_ref[...] = u_ref[...] * (vn * wd_ref[...] + b0_ref[...])

    @pl.when(i == N_PROMPT_CHUNKS)
    def _():
        vn_ref[...] = vn


def _sgu(proj, lng, lnb, ws, bsb, wd, b0):
    return pl.pallas_call(
        _sgu_kernel,
        grid=(N_PAD // CHUNK,),
        in_specs=[
            pl.BlockSpec((CHUNK, A_WIDTH), lambda i: (i, 0)),
            pl.BlockSpec((CHUNK, A_WIDTH), lambda i: (i, 1)),
            pl.BlockSpec((1, A_WIDTH), lambda i: (0, 0)),
            pl.BlockSpec((1, A_WIDTH), lambda i: (0, 0)),
            pl.BlockSpec((A_GROUPS, CHUNK, CHUNK), lambda i: (0, 0, 0)),
            pl.BlockSpec((A_GROUPS, CHUNK, CHUNK), lambda i: (0, 0, 0)),
            pl.BlockSpec((1, A_WIDTH), lambda i: (0, 0)),
            pl.BlockSpec((1, A_WIDTH), lambda i: (0, 0)),
        ],
        out_specs=[
            pl.BlockSpec((CHUNK, A_WIDTH), lambda i: (i, 0)),
            pl.BlockSpec((DEC_BATCH, A_WIDTH), lambda i: (0, 0)),
        ],
        out_shape=[
            jax.ShapeDtypeStruct((N_PAD, A_WIDTH), F32),
            jax.ShapeDtypeStruct((DEC_BATCH, A_WIDTH), F32),
        ],
        compiler_params=_params(("arbitrary",)),
        name="sgu",
    )(proj, proj, lng, lnb, ws, bsb, wd, b0)


Q_BLOCK = 128


def _softmax_pv(q, k, v, valid):
    s = lax.dot_general(q, k, (((1,), (1,)), ((), ())), preferred_element_type=F32) * ATTN_SCALE
    s = jnp.where(valid, s, -jnp.inf)
    m = jnp.max(s, axis=-1, keepdims=True)
    e = jnp.exp(s - m)
    den = jnp.sum(e, axis=-1, keepdims=True)
    o = jnp.dot((e / den).astype(BF16), v, preferred_element_type=F32)
    return o, m + jnp.log(den)


STRIDED_GROUPS = tuple(gi for gi, (_, d) in enumerate(DIL_GROUPS) if d > 1)
assert STRIDED_GROUPS == (1, 2) and all(w // d == Q_BLOCK for w, d in DIL_GROUPS)
STRIDED_UNROLL = 4


def _attn_strided_kernel(dil, q_ref, k_ref, v_ref, o_ref, l_ref):
    n_blk = SEQ // dil // Q_BLOCK
    n_keys = Q_BLOCK * min(n_blk, 2)
    dist0 = (lax.broadcasted_iota(jnp.int32, (Q_BLOCK, n_keys), 0)
             - lax.broadcasted_iota(jnp.int32, (Q_BLOCK, n_keys), 1))

    def tiles(it, carry):
        for u in range(STRIDED_UNROLL):
            tile = it * STRIDED_UNROLL + u
            r, n = tile // n_blk, tile % n_blk
            if n_blk == STRIDED_UNROLL:
                r, n = it, u
            first_key_blk = jnp.maximum(n - 1, 0)
            q_rows = pl.ds(r + dil * Q_BLOCK * n, Q_BLOCK, stride=dil)
            k_rows = pl.ds(r + dil * Q_BLOCK * first_key_blk, n_keys, stride=dil)
            dist = dist0 + Q_BLOCK * (n - first_key_blk)
            valid = jnp.where(dist >= 0, 1.0, 0.0) * jnp.where(dist <= Q_BLOCK, 1.0, 0.0) > 0.5
            o, lse = _softmax_pv(q_ref[q_rows, :].astype(BF16), k_ref[k_rows, :].astype(BF16),
                                 v_ref[k_rows, :].astype(BF16), valid)
            o_ref[q_rows, :] = o
            l_ref[q_rows, :] = jnp.broadcast_to(lse, (Q_BLOCK, HEAD_DIM))
        return carry

    lax.fori_loop(0, dil * n_blk // STRIDED_UNROLL, tiles, 0)


def _attn_strided(proj, gi):
    dil = DIL_GROUPS[gi][1]
    assert (SEQ // dil) % Q_BLOCK == 0 and (dil * (SEQ // dil // Q_BLOCK)) % STRIDED_UNROLL == 0
    heads_per_blk = PROJ_TN // HEAD_DIM
    src = lambda col: pl.BlockSpec(
        (SEQ, HEAD_DIM), lambda b, h: (b, (col + gi) * heads_per_blk + h))
    out = pl.BlockSpec((SEQ, HEAD_DIM), lambda b, h: (b, h))
    shape = jax.ShapeDtypeStruct((N_PROMPT, GROUP_WIDTH), F32)
    return pl.pallas_call(
        functools.partial(_attn_strided_kernel, dil),
        grid=(BATCH, HEADS_PER_GROUP),
        in_specs=[src(COL_Q), src(COL_K), src(COL_V)],
        out_specs=[out, out],
        out_shape=[shape, shape],
        compiler_params=_params(("parallel", "parallel")),
        name="attn_strided",
    )(proj, proj, proj)


def _attn_prompt_kernel(q_ref, k_ref, v_ref, o1_ref, l1_ref, o2_ref, l2_ref, o_ref):
    window = DIL_GROUPS[0][0]
    t0 = pl.multiple_of(pl.program_id(1) * Q_BLOCK, Q_BLOCK)
    n_keys = window + Q_BLOCK
    start = pl.multiple_of(jnp.maximum(t0 - window, 0), Q_BLOCK)
    dist = ((t0 - start) + lax.broadcasted_iota(jnp.int32, (Q_BLOCK, n_keys), 0)
            - lax.broadcasted_iota(jnp.int32, (Q_BLOCK, n_keys), 1))
    valid = jnp.where(dist >= 0, 1.0, 0.0) * jnp.where(dist <= window, 1.0, 0.0) > 0.5
    for h in range(HEADS_PER_GROUP):
        cs = slice(h * HEAD_DIM, (h + 1) * HEAD_DIM)
        o0, l0 = _softmax_pv(q_ref[:, cs].astype(BF16), k_ref[pl.ds(start, n_keys), cs].astype(BF16),
                             v_ref[pl.ds(start, n_keys), cs].astype(BF16), valid)
        l1, l2 = l1_ref[:, cs], l2_ref[:, cs]
        top = jnp.maximum(jnp.maximum(l0, l1), l2)
        w0, w1, w2 = jnp.exp(l0 - top), jnp.exp(l1 - top), jnp.exp(l2 - top)
        o_ref[:, cs] = (o0 * w0 + o1_ref[:, cs] * w1 + o2_ref[:, cs] * w2) / (w0 + w1 + w2)


def _attn_prompt(proj, strided):
    n_qb = SEQ // Q_BLOCK
    tile = pl.BlockSpec((Q_BLOCK, GROUP_WIDTH), lambda b, qb: (b * n_qb + qb, 0))
    (o1, l1), (o2, l2) = strided
    return pl.pallas_call(
        _attn_prompt_kernel,
        grid=(BATCH, n_qb),
        in_specs=[
            pl.BlockSpec((Q_BLOCK, GROUP_WIDTH), lambda b, qb: (b * n_qb + qb, COL_Q)),
            pl.BlockSpec((SEQ, GROUP_WIDTH), lambda b, qb: (b, COL_K)),
            pl.BlockSpec((SEQ, GROUP_WIDTH), lambda b, qb: (b, COL_V)),
            tile, tile, tile, tile,
        ],
        out_specs=tile,
        out_shape=jax.ShapeDtypeStruct((N_PROMPT, GROUP_WIDTH), F32),
        compiler_params=_params(("parallel", "arbitrary")),
        name="attn_prompt",
    )(proj, proj, proj, o1, l1, o2, l2)


SAMPLE_NB = 4


def _attn_sample_kernel(q_ref, kn_ref, vn_ref, k0_ref, v0_ref, k1_ref, v1_ref, k2_ref, v2_ref, o_ref):
    caches = ((k0_ref, v0_ref), (k1_ref, v1_ref), (k2_ref, v2_ref))

    def seq(n, carry):
        outs = []
        lses = []
        for g, (kc_ref, vc_ref) in enumerate(caches):
            q = q_ref[n, g]
            kc = kc_ref[n]
            s = jnp.sum(kc * q[None], axis=-1, keepdims=True) * ATTN_SCALE
            sn = jnp.sum(kn_ref[n, g] * q, axis=-1, keepdims=True) * ATTN_SCALE
            m = jnp.maximum(jnp.max(s, axis=0), sn)
            e = jnp.exp(s - m[None])
            en = jnp.exp(sn - m)
            den = jnp.sum(e, axis=0) + en
            o = (jnp.sum(e * vc_ref[n], axis=0) + en * vn_ref[n, g]) / den
            outs.append(o)
            lses.append(m + jnp.log(den))
        mx = jnp.maximum(jnp.maximum(lses[0], lses[1]), lses[2])
        ws = [jnp.exp(l - mx) for l in lses]
        tot = ws[0] + ws[1] + ws[2]
        o_ref[n] = (ws[0] * outs[0] + ws[1] * outs[1] + ws[2] * outs[2]) / tot
        return carry

    lax.fori_loop(0, SAMPLE_NB, seq, 0)


def _attn_sample(qs, kns, vns, caches):
    nb = SAMPLE_NB
    small = pl.BlockSpec((nb, len(DIL_GROUPS), HEADS_PER_GROUP, HEAD_DIM), lambda n: (n, 0, 0, 0))
    in_specs = [small, small, small]
    args = [qs, kns, vns]
    for c in caches:
        for kv in (0, 1):
            in_specs.append(pl.BlockSpec(
                (nb, N_KEYS, None, None, HEADS_PER_GROUP, HEAD_DIM),
                functools.partial(lambda n, kv_: (n, 0, 0, kv_, 0, 0), kv_=kv)))
            args.append(c)
    return pl.pallas_call(
        _attn_sample_kernel,
        grid=(DEC_BATCH // nb,),
        in_specs=in_specs,
        out_specs=pl.BlockSpec((nb, HEADS_PER_GROUP, HEAD_DIM), lambda n: (n, 0, 0)),
        out_shape=jax.ShapeDtypeStruct((DEC_BATCH, HEADS_PER_GROUP, HEAD_DIM), F32),
        compiler_params=_params(("parallel",)),
        name="attn_sample",
    )(*args)


TAIL_ROWS = N_PAD - N_PROMPT
MERGE_TM = TAIL_ROWS
N_PROMPT_BLOCKS = N_PROMPT // TAIL_ROWS


def _prompt_or_tail_specs(width):
    return [pl.BlockSpec((TAIL_ROWS, width), lambda i: (jnp.minimum(i, N_PROMPT_BLOCKS - 1), 0)),
            pl.BlockSpec((TAIL_ROWS, width), lambda i: (0, 0))]


def _prompt_or_tail(prompt_ref, tail_ref):
    return jnp.where(pl.program_id(0) < N_PROMPT_BLOCKS, prompt_ref[...], tail_ref[...])


def _merge_kernel(xp_ref, xt_ref, a_ref, bp_ref, bt_ref, ga_ref, gb_ref, wa_ref, wb_ref, wo_ref, o_ref):
    pa = jnp.dot(a_ref[...].astype(BF16), wa_ref[...], preferred_element_type=F32)
    pb = jnp.dot(_prompt_or_tail(bp_ref, bt_ref).astype(BF16), wb_ref[...], preferred_element_type=F32)
    merged = ga_ref[...] * pa + gb_ref[...] * pb
    o_ref[...] = _prompt_or_tail(xp_ref, xt_ref) + jnp.dot(merged.astype(BF16), wo_ref[...],
                                                          preferred_element_type=F32)


def _merge(x_prompt, x_tail, amix, bmix_prompt, bmix_tail, proj, wa, wb, wo):
    tm = MERGE_TM
    gate_a_blk = COL_GATE * PROJ_TN // D_MODEL
    return pl.pallas_call(
        _merge_kernel,
        grid=(N_PAD // tm,),
        in_specs=[
            *_prompt_or_tail_specs(D_MODEL),
            pl.BlockSpec((tm, A_WIDTH), lambda i: (i, 0)),
            *_prompt_or_tail_specs(GROUP_WIDTH),
            pl.BlockSpec((tm, D_MODEL), lambda i: (i, gate_a_blk)),
            pl.BlockSpec((tm, D_MODEL), lambda i: (i, gate_a_blk + 1)),
            _const_spec((A_WIDTH, D_MODEL)),
            _const_spec((GROUP_WIDTH, D_MODEL)),
            _const_spec((D_MODEL, D_MODEL)),
        ],
        out_specs=pl.BlockSpec((tm, D_MODEL), lambda i: (i, 0)),
        out_shape=jax.ShapeDtypeStruct((N_PAD, D_MODEL), F32),
        compiler_params=_params(("parallel",)),
        name="merge",
    )(x_prompt, x_tail, amix, bmix_prompt, bmix_tail, proj, proj, wa, wb, wo)


GATE_TB = 128
PAIRS_PER_RANK = tuple(PEER_TOPK // (a + 1) for a in range(PEER_TOPK))


SUBLANES = 8


def _batcher_pairs(n):
    pairs = []

    def merge(lo, m, r):
        step = 2 * r
        if step < m:
            merge(lo, m, step)
            merge(lo + r, m, step)
            pairs.extend((i, i + r) for i in range(lo + r, lo + m - r, step))
        else:
            pairs.append((lo, lo + r))

    def sort(lo, m):
        if m > 1:
            sort(lo, m // 2)
            sort(lo + m // 2, m // 2)
            merge(lo, m, 1)

    sort(0, n)
    return tuple(pairs)


SORT16_PAIRS = _batcher_pairs(PEER_TOPK)


def _exchange(v, i, j):
    v[i], v[j] = jnp.maximum(v[i], v[j]), jnp.minimum(v[i], v[j])


def _bitonic_sort_desc(c):
    c = list(c)
    d = PEER_TOPK // 2
    while d:
        for k in range(PEER_TOPK):
            if not k & d:
                _exchange(c, k, k + d)
        d //= 2
    return c


def _merge_sublanes(v, sort_last):
    for shift in (4, 2, 1):
        other = [pltpu.roll(x, shift, 0) for x in v]
        c = [jnp.maximum(v[k], other[PEER_TOPK - 1 - k]) for k in range(PEER_TOPK)]
        v = c if (shift == 1 and not sort_last) else _bitonic_sort_desc(c)
    return v


def _top16_sorted(s):
    v = [s[k * SUBLANES:(k + 1) * SUBLANES] for k in range(s.shape[0] // SUBLANES)]
    for i, j in SORT16_PAIRS:
        _exchange(v, i, j)
    return _merge_sublanes(v, True)


def _by_sublane(rows, sub8):
    out = rows[SUBLANES - 1]
    for b in range(SUBLANES - 2, -1, -1):
        out = jnp.where(sub8 == b, rows[b], out)
    return out


def _cast_once(pairs):
    @pl.when(pl.program_id(0) == 0)
    def _():
        for src_ref, dst_ref in pairs:
            dst_ref[...] = src_ref[...].astype(BF16)


def _peer_gate_kernel(x_ref, g_ref, wq32_ref, k132_ref, k232_ref,
                      xnt_ref, s1_ref, a_ref, s2_ref, bn_ref, tau_ref, wq_ref, k1_ref, k2_ref):
    _cast_once(((wq32_ref, wq_ref), (k132_ref, k1_ref), (k232_ref, k2_ref)))
    xn = _rms(x_ref[...], g_ref[...])
    xnt_ref[...] = xn.T.astype(BF16)
    qp = jnp.dot(xn.astype(BF16), wq_ref[...], preferred_element_type=F32).astype(BF16)
    sub8 = lax.broadcasted_iota(jnp.int32, (8, GATE_TB), 0)
    nt = (((1,), (1,)), ((), ()))
    for h in range(PEER_HEADS):
        c0 = h * 2 * PEER_HALF
        s1 = lax.dot_general(k1_ref[...], qp[:, c0:c0 + PEER_HALF], nt, preferred_element_type=F32)
        s2 = lax.dot_general(k2_ref[...], qp[:, c0 + PEER_HALF:c0 + 2 * PEER_HALF], nt,
                             preferred_element_type=F32)
        t1 = _top16_sorted(s1)
        t2 = _top16_sorted(s2)
        t2_lo = _by_sublane(t2[:SUBLANES], sub8)
        t2_hi = _by_sublane(t2[SUBLANES:], sub8)
        lo = []
        for a in range(PEER_TOPK):
            nvalid = min(SUBLANES, PAIRS_PER_RANK[a])
            c = t1[a] + t2_lo
            lo.append(c if nvalid == SUBLANES else jnp.where(sub8 < nvalid, c, -jnp.inf))
        hi = t1[0] + t2_hi
        ins = [jnp.maximum(lo[0], hi)]
        ins += [jnp.maximum(lo[k], jnp.minimum(lo[k - 1], hi)) for k in range(1, PEER_TOPK)]
        best = _merge_sublanes(ins, False)
        tau = functools.reduce(jnp.minimum, best)
        e2_lo = jnp.exp(t2_lo - t2[0])
        zacc = jnp.where(hi >= tau, jnp.exp(t2_hi - t2[0]), 0.0)
        for a in range(PEER_TOPK):
            zacc = zacc + jnp.where(lo[a] >= tau, jnp.exp(t1[a] - t1[0]) * e2_lo, 0.0)
        z = jnp.sum(zacc, axis=0, keepdims=True)
        s1_ref[h] = s1
        s2_ref[h] = s2
        a_ref[h] = jnp.exp(s1 - t1[0][0:1])
        bn_ref[h] = jnp.exp(s2 - t2[0][0:1]) / z
        tau_ref[h] = tau[0:1]


def _peer_gate(x1, g, wq, k1, k2):
    tb = GATE_TB
    big = pl.BlockSpec((PEER_HEADS, N_KEYS, tb), lambda i: (0, 0, i))
    big_shape = jax.ShapeDtypeStruct((PEER_HEADS, N_KEYS, N_PAD), F32)
    return pl.pallas_call(
        _peer_gate_kernel,
        grid=(N_PAD // tb,),
        in_specs=[
            pl.BlockSpec((tb, D_MODEL), lambda i: (i, 0)),
            pl.BlockSpec((1, D_MODEL), lambda i: (0, 0)),
            _const_spec((D_MODEL, PEER_HEADS * 2 * PEER_HALF)),
            _const_spec((N_KEYS, PEER_HALF)),
            _const_spec((N_KEYS, PEER_HALF)),
        ],
        out_specs=[
            pl.BlockSpec((D_MODEL, tb), lambda i: (0, i)),
            big, big, big, big,
            pl.BlockSpec((PEER_HEADS, 1, tb), lambda i: (0, 0, i)),
        ],
        out_shape=[
            jax.ShapeDtypeStruct((D_MODEL, N_PAD), BF16),
            big_shape, big_shape, big_shape, big_shape,
            jax.ShapeDtypeStruct((PEER_HEADS, 1, N_PAD), F32),
        ],
        scratch_shapes=[pltpu.VMEM((D_MODEL, PEER_HEADS * 2 * PEER_HALF), BF16),
                        pltpu.VMEM((N_KEYS, PEER_HALF), BF16), pltpu.VMEM((N_KEYS, PEER_HALF), BF16)],
        compiler_params=_params(("arbitrary",)),
        name="peer_gate",
    )(x1, g, wq, k1, k2)


DENSE_TB = 768
DENSE_EB = 1024
DENSE_EB_F32 = 512
N_DENSE_TOKEN_BLOCKS = N_PAD // DENSE_TB


def _peer_dense_kernel(from_f32, xnt_ref, s1_ref, a_ref, s2_ref, bn_ref, tau_ref, u_ref, v_ref, *rest):
    if from_f32:
        o_ref, ub_ref, vtb_ref, st_ref, wt_ref = rest
        u = u_ref[...].astype(BF16)
        v = v_ref[...].astype(BF16)
        ub_ref[...] = u
        vtb_ref[...] = v.T
    else:
        o_ref, st_ref, wt_ref = rest
    n_first = st_ref.shape[0] // N_KEYS

    @pl.when(pl.program_id(1) == 0)
    def _():
        o_ref[...] = jnp.zeros_like(o_ref)

    st_ref[...] = jnp.dot(u if from_f32 else u_ref[...], xnt_ref[...], preferred_element_type=F32)

    for ii in range(n_first):
        rows = slice(ii * N_KEYS, (ii + 1) * N_KEYS)
        for c in range(DENSE_TB // LANE):
            lanes = slice(c * LANE, (c + 1) * LANE)
            gate = jnp.zeros((N_KEYS, LANE), F32)
            for h in range(PEER_HEADS):
                s1_row = s1_ref[h, ii:ii + 1, lanes]
                a_row = a_ref[h, ii:ii + 1, lanes]
                hit = (s1_row + s2_ref[h, :, lanes]) >= tau_ref[h, :, lanes]
                gate = gate + jnp.where(hit, a_row * bn_ref[h, :, lanes], 0.0)
            act = jax.nn.gelu(st_ref[rows, lanes])
            wt_ref[rows, lanes] = (gate * act).astype(BF16)

    if from_f32:
        o_ref[...] += lax.dot_general(v, wt_ref[...], (((0,), (0,)), ((), ())), preferred_element_type=F32)
    else:
        o_ref[...] += jnp.dot(v_ref[...], wt_ref[...], preferred_element_type=F32)


def _peer_dense(xnt, s1t, at, s2t, bnt, tau, u_f32, v_f32):
    tb = DENSE_TB
    once = pl.Buffered(1)

    def token_specs(t_of):
        per_tok = lambda shape: pl.BlockSpec(shape, lambda t, e: (0,) * (len(shape) - 1) + (t_of(t),),
                                             pipeline_mode=once)
        return [per_tok((D_MODEL, tb)), per_tok((PEER_HEADS, N_KEYS, tb)), per_tok((PEER_HEADS, N_KEYS, tb)),
                per_tok((PEER_HEADS, 1, tb))]

    eb = DENSE_EB_F32
    ni = eb // N_KEYS
    n_e = N_EXPERTS // eb
    halves = DENSE_EB // eb
    first_keys = lambda a: a[:, :, :tb].reshape(PEER_HEADS, n_e, ni, tb)
    by_key0 = pl.BlockSpec((PEER_HEADS, None, ni, tb), lambda t, e: (0, e, 0, 0))
    xnt_s, s2_s, bn_s, tau_s = token_specs(lambda t: 0)
    out0, u_bf, vt_bf = pl.pallas_call(
        functools.partial(_peer_dense_kernel, True),
        grid=(1, n_e),
        in_specs=[xnt_s, by_key0, by_key0, s2_s, bn_s, tau_s,
                  pl.BlockSpec((eb, D_MODEL), lambda t, e: (e, 0)),
                  pl.BlockSpec((eb, D_MODEL), lambda t, e: (e, 0))],
        out_specs=[
            pl.BlockSpec((D_MODEL, tb), lambda t, e: (0, 0), pipeline_mode=once),
            pl.BlockSpec((eb, D_MODEL), lambda t, e: (e, 0)),
            pl.BlockSpec((None, D_MODEL, eb), lambda t, e: (e // halves, 0, e % halves)),
        ],
        out_shape=[
            jax.ShapeDtypeStruct((D_MODEL, tb), F32),
            jax.ShapeDtypeStruct((N_EXPERTS, D_MODEL), BF16),
            jax.ShapeDtypeStruct((N_EXPERTS // DENSE_EB, D_MODEL, DENSE_EB), BF16),
        ],
        scratch_shapes=[pltpu.VMEM((eb, tb), F32), pltpu.VMEM((eb, tb), BF16)],
        compiler_params=_params(("arbitrary", "arbitrary")),
        name="peer_dense_first",
    )(xnt, first_keys(s1t), first_keys(at), s2t, bnt, tau, u_f32, v_f32)

    eb = DENSE_EB
    by_key = pl.BlockSpec((PEER_HEADS, eb // N_KEYS, tb), lambda t, e: (0, e, t + 1))
    xnt_s, s2_s, bn_s, tau_s = token_specs(lambda t: t + 1)
    out_rest = pl.pallas_call(
        functools.partial(_peer_dense_kernel, False),
        grid=(N_DENSE_TOKEN_BLOCKS - 1, N_EXPERTS // eb),
        in_specs=[xnt_s, by_key, by_key, s2_s, bn_s, tau_s,
                  pl.BlockSpec((eb, D_MODEL), lambda t, e: (e, 0)),
                  pl.BlockSpec((None, D_MODEL, eb), lambda t, e: (e, 0, 0))],
        out_specs=pl.BlockSpec((D_MODEL, tb), lambda t, e: (0, t)),
        out_shape=jax.ShapeDtypeStruct((D_MODEL, N_PAD - tb), F32),
        scratch_shapes=[pltpu.VMEM((eb, tb), F32), pltpu.VMEM((eb, tb), BF16)],
        compiler_params=_params(("parallel", "arbitrary")),
        name="peer_dense",
    )(xnt, s1t, at, s2t, bnt, tau, u_bf, vt_bf)
    return out0, out_rest


FINAL_TM = 256
N_PROMPT_FINAL_BLOCKS = N_PROMPT // FINAL_TM


FINAL_FIRST_BLOCKS = DENSE_TB // FINAL_TM


def _final_kernel(x_ref, pt0_ref, pt_ref, pp_ref, ptail_ref, wg32_ref, wp32_ref, g_ref, yp_ref, ys_ref,
                  wg_ref, wp_ref):
    _cast_once(((wg32_ref, wg_ref), (wp32_ref, wp_ref)))
    i = pl.program_id(0)
    peer_t = jnp.where(i < FINAL_FIRST_BLOCKS, pt0_ref[...], pt_ref[...])
    x2 = x_ref[...] + peer_t.T
    gate = _sigmoid(jnp.dot(x2.astype(BF16), wg_ref[...], preferred_element_type=F32))
    emb = jnp.dot(_prompt_or_tail(pp_ref, ptail_ref).astype(BF16), wp_ref[...], preferred_element_type=F32)
    y = _rms(x2 + gate * emb, g_ref[...])

    @pl.when(i < N_PROMPT_FINAL_BLOCKS)
    def _():
        yp_ref[...] = y

    @pl.when(i == N_PROMPT_FINAL_BLOCKS)
    def _():
        ys_ref[...] = y[:DEC_BATCH]


def _final(x1, peer_t0, peer_t, p_prompt, p_tail, wg, wp, g):
    tm = FINAL_TM
    assert N_PAD // tm == N_PROMPT_FINAL_BLOCKS + 1 and DENSE_TB % tm == 0 and tm == TAIL_ROWS
    nf = FINAL_FIRST_BLOCKS
    return pl.pallas_call(
        _final_kernel,
        grid=(N_PAD // tm,),
        in_specs=[
            pl.BlockSpec((tm, D_MODEL), lambda i: (i, 0)),
            pl.BlockSpec((D_MODEL, tm), lambda i: (0, jnp.minimum(i, nf - 1))),
            pl.BlockSpec((D_MODEL, tm), lambda i: (0, jnp.maximum(i - nf, 0))),
            *_prompt_or_tail_specs(PLE_DIM),
            _const_spec((D_MODEL, D_MODEL)),
            _const_spec((PLE_DIM, D_MODEL)),
            pl.BlockSpec((1, D_MODEL), lambda i: (0, 0)),
        ],
        out_specs=[
            pl.BlockSpec((tm, D_MODEL), lambda i: (jnp.minimum(i, N_PROMPT_FINAL_BLOCKS - 1), 0)),
            pl.BlockSpec((DEC_BATCH, D_MODEL), lambda i: (0, 0)),
        ],
        out_shape=[
            jax.ShapeDtypeStruct((N_PROMPT, D_MODEL), F32),
            jax.ShapeDtypeStruct((DEC_BATCH, D_MODEL), F32),
        ],
        scratch_shapes=[pltpu.VMEM((D_MODEL, D_MODEL), BF16), pltpu.VMEM((PLE_DIM, D_MODEL), BF16)],
        compiler_params=_params(("arbitrary",)),
        name="final",
    )(x1, peer_t0, peer_t, p_prompt, p_tail, wg, wp, g)


KV_ROWS = 512


def _kv_store(k, v, o_ref):
    for h in range(HEADS_PER_GROUP):
        cs = slice(h * HEAD_DIM, (h + 1) * HEAD_DIM)
        o_ref[:, 0, h, :] = k[:, cs]
        o_ref[:, 1, h, :] = v[:, cs]


def _kv_prompt_kernel(k_ref, v_ref, o_ref):
    _kv_store(k_ref[...], v_ref[...], o_ref)


def _kv_prompt(proj, gi, rows):
    blk = min(rows, KV_ROWS)
    nblk = rows // blk
    first = SEQ // blk - nblk
    src = lambda col: pl.BlockSpec(
        (blk, GROUP_WIDTH), lambda b, r: (b * (SEQ // blk) + first + r, col + gi))
    return pl.pallas_call(
        _kv_prompt_kernel,
        grid=(BATCH, nblk),
        in_specs=[src(COL_K), src(COL_V)],
        out_specs=pl.BlockSpec((None, None, blk, 2, HEADS_PER_GROUP, HEAD_DIM),
                               lambda b, r: (0, b, r, 0, 0, 0)),
        out_shape=jax.ShapeDtypeStruct((1, BATCH, rows, 2, HEADS_PER_GROUP, HEAD_DIM), F32),
        compiler_params=_params(("parallel", "parallel")),
        name="kv_prompt",
    )(proj, proj)


def _kv_sample_kernel(k0_ref, v0_ref, k1_ref, v1_ref, k2_ref, v2_ref, o0_ref, o1_ref, o2_ref):
    _kv_store(k0_ref[...], v0_ref[...], o0_ref)
    _kv_store(k1_ref[...], v1_ref[...], o1_ref)
    _kv_store(k2_ref[...], v2_ref[...], o2_ref)


def _kv_sample(proj):
    row_blk = N_PROMPT // DEC_BATCH
    src = lambda col: pl.BlockSpec((DEC_BATCH, GROUP_WIDTH), lambda i: (row_blk, col))
    in_specs = []
    for gi in range(len(DIL_GROUPS)):
        in_specs += [src(COL_K + gi), src(COL_V + gi)]
    out_spec = pl.BlockSpec((None, DEC_BATCH, None, 2, HEADS_PER_GROUP, HEAD_DIM),
                            lambda i: (0, 0, 0, 0, 0, 0))
    out_shape = jax.ShapeDtypeStruct((1, DEC_BATCH, 1, 2, HEADS_PER_GROUP, HEAD_DIM), F32)
    return pl.pallas_call(
        _kv_sample_kernel,
        grid=(1,),
        in_specs=in_specs,
        out_specs=[out_spec] * 3,
        out_shape=[out_shape] * 3,
        compiler_params=_params(("arbitrary",)),
        name="kv_sample",
    )(*([proj] * 6))


def _rope_tables():
    pos = jnp.concatenate([
        jnp.tile(jnp.arange(SEQ, dtype=jnp.int32), BATCH),
        jnp.full((DEC_BATCH,), PAST_LEN, jnp.int32),
        jnp.zeros((N_PAD - N_TOK,), jnp.int32)])
    inv = ROPE_THETA ** (-jnp.arange(ROT_HALF, dtype=F32) / ROT_HALF)
    ang = pos.astype(F32)[:, None] * inv[None, :]
    cos, sin = jnp.cos(ang), jnp.sin(ang)
    n = pos.shape[0]
    zeros = lambda w: jnp.zeros((n, w), F32)
    c = jnp.concatenate([cos, cos, jnp.ones((n, HEAD_DIM - 2 * ROT_HALF), F32)], axis=1)
    s_hi = jnp.concatenate([-sin, zeros(HEAD_DIM - ROT_HALF)], axis=1)
    s_lo = jnp.concatenate([zeros(ROT_HALF), sin, zeros(HEAD_DIM - 2 * ROT_HALF)], axis=1)
    return c, s_hi, s_lo


def kernel(x_prompt, x_sample, cache_kv_w128, cache_kv_w512, cache_kv_w2048, p_prompt, p_sample, g_mix, w_in, sgu_ln_g, sgu_ln_b, w_s, b_s, w_a_out, w_b_out, w_o, g_ffn, peer_w_q, peer_sub_k1, peer_sub_k2, peer_u, peer_v, w_ple, w_ple_gate, g_final):
    assert x_prompt.shape == (BATCH, SEQ, D_MODEL) and x_sample.shape == (DEC_BATCH, 1, D_MODEL)
    assert w_in.shape == (1, D_MODEL, IN_COLS)
    pad = N_PAD - N_TOK
    tail = lambda a: jnp.pad(a.reshape(DEC_BATCH, -1), ((0, pad), (0, 0)))
    x_p, x_t = x_prompt.reshape(N_PROMPT, D_MODEL), tail(x_sample)
    p_p, p_t = p_prompt.reshape(N_PROMPT, PLE_DIM), tail(p_sample)

    c, s_hi, s_lo = _rope_tables()
    proj = _proj(x_p, x_t, g_mix, w_in[0], c, s_hi, s_lo)

    bsb = jnp.broadcast_to(b_s[0][:, :, None], (A_GROUPS, CHUNK, CHUNK))
    wd = jnp.repeat(w_s[0, :, 0, 0], CHUNK)[None]
    b0 = jnp.repeat(b_s[0, :, 0], CHUNK)[None]
    amix, vn_s = _sgu(proj, sgu_ln_g, sgu_ln_b, w_s[0], bsb, wd, b0)

    bmix_p = _attn_prompt(proj, [_attn_strided(proj, gi) for gi in STRIDED_GROUPS])
    cq, ck, cv = COL_Q * PROJ_TN, COL_K * PROJ_TN, COL_V * PROJ_TN
    srows = proj[N_PROMPT:N_TOK]
    per_head = lambda a: a.reshape(DEC_BATCH, len(DIL_GROUPS), HEADS_PER_GROUP, HEAD_DIM)
    caches = []
    for cache, (window, dil) in zip((cache_kv_w128, cache_kv_w512, cache_kv_w2048), DIL_GROUPS):
        l_buf = cache.shape[2]
        assert l_buf == N_KEYS * dil
        caches.append(cache.reshape(DEC_BATCH, N_KEYS, dil, 2, HEADS_PER_GROUP, HEAD_DIM))
    bmix_s = _attn_sample(per_head(srows[:, cq:ck]), per_head(srows[:, ck:cv]), per_head(srows[:, cv:cv + B_WIDTH]), caches)
    x1 = _merge(x_p, x_t, amix, bmix_p, tail(bmix_s), proj, w_a_out[0].astype(BF16), w_b_out[0].astype(BF16), w_o[0].astype(BF16))

    xnt, s1t, at, s2t, bnt, tau = _peer_gate(x1, g_ffn, peer_w_q[0], peer_sub_k1[0], peer_sub_k2[0])
    peer_t0, peer_t = _peer_dense(xnt, s1t, at, s2t, bnt, tau, peer_u[0], peer_v[0])

    y_p, y_s = _final(x1, peer_t0, peer_t, p_p, p_t, w_ple_gate[0], w_ple[0], g_final[None])

    y_prompt = y_p.reshape(BATCH, SEQ, D_MODEL)
    y_sample = y_s.reshape(DEC_BATCH, 1, D_MODEL)
    kv_p = [_kv_prompt(proj, gi, min(window, SEQ)) for gi, (window, _) in enumerate(DIL_GROUPS)]
    kv_s = _kv_sample(proj)
    sgu_v_sample = vn_s.reshape(1, DEC_BATCH, 1, A_WIDTH)
    return (y_prompt, y_sample, kv_p[0], kv_p[1], kv_p[2], kv_s[0], kv_s[1], kv_s[2], sgu_v_sample)
```

```python
import functools

import jax
import numpy as np
import jax.numpy as jnp
from jax import lax
from jax.experimental import pallas as pl
from jax.experimental.pallas import tpu as pltpu

F32 = jnp.float32
BF16 = jnp.bfloat16

D_MODEL = 2048
BATCH = 4
SEQ = 2048
DEC_BATCH = 128
PAST_LEN = 2048
EPS = 1e-6
CHUNK = 128
A_GROUPS = 8
A_WIDTH = 1024
HEAD_DIM = 128
HEADS_PER_GROUP = 4
DIL_GROUPS = ((128, 1), (512, 4), (2048, 16))
GROUP_WIDTH = HEADS_PER_GROUP * HEAD_DIM
B_WIDTH = 3 * GROUP_WIDTH
ATTN_SCALE = HEAD_DIM ** -0.5
ROPE_THETA = 500000.0
ROT_HALF = HEAD_DIM // 8
IN_COLS = 2 * A_WIDTH + 3 * B_WIDTH + 2 * D_MODEL
N_KEYS = 128
N_EXPERTS = N_KEYS * N_KEYS
PEER_HEADS = 8
PEER_HALF = 128
PEER_TOPK = 16
PLE_DIM = 256

N_PROMPT = BATCH * SEQ
N_TOK = N_PROMPT + DEC_BATCH
N_PAD = 8448
LANE = 128

PROJ_TN = 512
COL_GATE = 2 * A_WIDTH // PROJ_TN
COL_Q = COL_GATE + 2 * D_MODEL // PROJ_TN
COL_K = COL_Q + B_WIDTH // PROJ_TN
COL_V = COL_K + B_WIDTH // PROJ_TN
N_COL_BLOCKS = IN_COLS // PROJ_TN

VMEM_LIMIT = 56 * 1024 * 1024


def _params(sem, vmem=VMEM_LIMIT):
    return pltpu.CompilerParams(dimension_semantics=sem, vmem_limit_bytes=vmem)


def _const_spec(shape):
    nd = len(shape)
    return pl.BlockSpec(shape, lambda *_: (0,) * nd, pipeline_mode=pl.Buffered(1))


def _rms(x, g):
    r = lax.rsqrt(jnp.mean(x * x, axis=-1, keepdims=True) + EPS)
    return (x * r) * g


def _sigmoid(x):
    return 1.0 / (1.0 + jnp.exp(-x))


PROJ_TM = 1024
PROJ_CHUNK = 256


PROJ_STEP_COLS = 2 * PROJ_TN
PROJ_COLS = -(-IN_COLS // PROJ_STEP_COLS) * PROJ_STEP_COLS


def _proj_kernel(cast_w, x_ref, g_ref, wa_ref, wb_ref, c_ref, s1_ref, s2_ref, o_ref, *rest):
    j = 2 * pl.program_id(1)
    if cast_w:
        w_out, h_ref = rest
        w_out[:, :PROJ_TN] = wa_ref[...].astype(BF16)
        w_out[:, PROJ_TN:] = wb_ref[...].astype(BF16)
        w_cols = lambda half, c0: w_out[:, half * PROJ_TN + c0:half * PROJ_TN + c0 + PROJ_CHUNK]
    else:
        h_ref, = rest
        w_cols = lambda half, c0: (wa_ref, wb_ref)[half][:, c0:c0 + PROJ_CHUNK]

    @pl.when(pl.program_id(1) == 0)
    def _():
        h_ref[...] = _rms(x_ref[...], g_ref[...]).astype(BF16)

    def rope(a):
        c = c_ref[...]
        s1 = s1_ref[...]
        s2 = s2_ref[...]
        heads = []
        for hh in range(a.shape[1] // HEAD_DIM):
            ah = a[:, hh * HEAD_DIM:(hh + 1) * HEAD_DIM]
            heads.append(ah * c + pltpu.roll(ah, HEAD_DIM - ROT_HALF, 1) * s1 + pltpu.roll(ah, ROT_HALF, 1) * s2)
        return jnp.concatenate(heads, axis=1)

    def project(epilogue):
        def branch():
            for half in range(2):
                for c0 in range(0, PROJ_TN, PROJ_CHUNK):
                    acc = jnp.dot(h_ref[...], w_cols(half, c0), preferred_element_type=F32)
                    o0 = half * PROJ_TN + c0
                    o_ref[:, o0:o0 + PROJ_CHUNK] = epilogue(acc)
        return branch

    pl.when(j < COL_GATE)(project(jax.nn.gelu))
    pl.when((j >= COL_GATE) & (j < COL_Q))(project(_sigmoid))
    pl.when((j >= COL_Q) & (j < COL_V))(project(rope))
    pl.when(j >= COL_V)(project(lambda a: a))


def _w_in_col_block(j):
    n_qkv = 3 * B_WIDTH // PROJ_TN
    n_gate = 2 * D_MODEL // PROJ_TN
    j = jnp.minimum(j, N_COL_BLOCKS - 1)
    return jnp.where(j < COL_GATE, j, jnp.where(j < COL_Q, j + n_qkv, j - n_gate))


def _proj_rows(x_rows, g, w, c, s1, s2, tm, first_blk, filled=None):
    n_blk = x_rows.shape[0] // tm
    cast_w = w.dtype == F32
    assert not (cast_w and (filled is not None or n_blk != 1))
    stored = lambda half: pl.BlockSpec((D_MODEL, PROJ_TN), lambda i, j: (0, 2 * j + half))
    if cast_w:
        w_half = lambda half: pl.BlockSpec((D_MODEL, PROJ_TN), lambda i, j: (0, _w_in_col_block(2 * j + half)))
    else:
        w_half = stored
    table = pl.BlockSpec((tm, HEAD_DIM), lambda i, j: (i + first_blk, 0))
    in_specs = [
        pl.BlockSpec((tm, D_MODEL), lambda i, j: (i, 0)),
        pl.BlockSpec((1, D_MODEL), lambda i, j: (0, 0)),
        w_half(0),
        w_half(1),
        table, table, table,
    ]
    args = [x_rows, g, w, w, c, s1, s2]
    kernel_fn, aliases = functools.partial(_proj_kernel, cast_w), {}
    if filled is not None:
        in_specs.append(pl.BlockSpec(memory_space=pl.ANY))
        args.append(filled)
        aliases = {len(args) - 1: 0}
        kernel_fn = lambda *refs: _proj_kernel(False, *refs[:7], *refs[8:])
    out_specs = [pl.BlockSpec((tm, PROJ_STEP_COLS), lambda i, j: (i + first_blk, j))]
    out_shape = [jax.ShapeDtypeStruct((N_PAD, PROJ_COLS), F32)]
    if cast_w:
        out_specs.append(pl.BlockSpec((D_MODEL, PROJ_STEP_COLS), lambda i, j: (0, j)))
        out_shape.append(jax.ShapeDtypeStruct((D_MODEL, PROJ_COLS), BF16))
    return pl.pallas_call(
        kernel_fn,
        grid=(n_blk, PROJ_COLS // PROJ_STEP_COLS),
        in_specs=in_specs,
        out_specs=out_specs,
        out_shape=out_shape,
        scratch_shapes=[pltpu.VMEM((tm, D_MODEL), BF16)],
        input_output_aliases=aliases,
        compiler_params=_params(("parallel", "arbitrary")),
        name="proj",
    )(*args)


def _proj(x_prompt, x_tail, g, w_f32, c, s1, s2):
    tail = x_tail.shape[0]
    assert N_PROMPT % PROJ_TM == 0 and N_PROMPT % tail == 0 and N_PROMPT + tail == N_PAD
    proj, w_bf = _proj_rows(x_tail, g, w_f32, c, s1, s2, tail, N_PROMPT // tail)
    return _proj_rows(x_prompt, g, w_bf, c, s1, s2, PROJ_TM, 0, filled=proj)[0]


N_PROMPT_CHUNKS = N_PROMPT // CHUNK


def _sgu_kernel(u_ref, gv_ref, lng_ref, lnb_ref, ws_ref, bsb_ref, wd_ref, b0_ref, o_ref, vn_ref):
    i = pl.program_id(0)
    gv = gv_ref[...]
    mu = jnp.mean(gv, axis=-1, keepdims=True)
    var = jnp.mean(jnp.square(gv - mu), axis=-1, keepdims=True)
    vn = ((gv - mu) * lax.rsqrt(var + EPS)) * lng_ref[...] + lnb_ref[...]

    @pl.when(i < N_PROMPT_CHUNKS)
    def _():
        row = lax.broadcasted_iota(jnp.int32, (CHUNK, CHUNK), 0)
        col = lax.broadcasted_iota(jnp.int32, (CHUNK, CHUNK), 1)
        causal = col <= row
        for g in range(A_GROUPS):
            cs = slice(g * CHUNK, (g + 1) * CHUNK)
            w = jnp.where(causal, ws_ref[g], 0.0).astype(BF16)
            mixed = jnp.dot(w, vn[:, cs].astype(BF16), preferred_element_type=F32) + bsb_ref[g]
            o_ref[:, cs] = u_ref[:, cs] * mixed

    @pl.when(i >= N_PROMPT_CHUNKS)
    def _():
        o_ref[...] = u_ref[...] * (vn * wd_ref[...] + b0_ref[...])

    @pl.when(i == N_PROMPT_CHUNKS)
    def _():
        vn_ref[...] = vn


def _sgu(proj, lng, lnb, ws, bsb, wd, b0):
    return pl.pallas_call(
        _sgu_kernel,
        grid=(N_PAD // CHUNK,),
        in_specs=[
            pl.BlockSpec((CHUNK, A_WIDTH), lambda i: (i, 0)),
            pl.BlockSpec((CHUNK, A_WIDTH), lambda i: (i, 1)),
            pl.BlockSpec((1, A_WIDTH), lambda i: (0, 0)),
            pl.BlockSpec((1, A_WIDTH), lambda i: (0, 0)),
            pl.BlockSpec((A_GROUPS, CHUNK, CHUNK), lambda i: (0, 0, 0)),
            pl.BlockSpec((A_GROUPS, CHUNK, CHUNK), lambda i: (0, 0, 0)),
            pl.BlockSpec((1, A_WIDTH), lambda i: (0, 0)),
            pl.BlockSpec((1, A_WIDTH), lambda i: (0, 0)),
        ],
        out_specs=[
            pl.BlockSpec((CHUNK, A_WIDTH), lambda i: (i, 0)),
            pl.BlockSpec((DEC_BATCH, A_WIDTH), lambda i: (0, 0)),
        ],
        out_shape=[
            jax.ShapeDtypeStruct((N_PAD, A_WIDTH), F32),
            jax.ShapeDtypeStruct((DEC_BATCH, A_WIDTH), F32),
        ],
        compiler_params=_params(("arbitrary",)),
        name="sgu",
    )(proj, proj, lng, lnb, ws, bsb, wd, b0)


Q_BLOCK = 128


def _softmax_pv(q, k, v, valid):
    s = lax.dot_general(q, k, (((1,), (1,)), ((), ())), preferred_element_type=F32) * ATTN_SCALE
    s = jnp.where(valid, s, -jnp.inf)
    m = jnp.max(s, axis=-1, keepdims=True)
    e = jnp.exp(s - m)
    den = jnp.sum(e, axis=-1, keepdims=True)
    o = jnp.dot((e / den).astype(BF16), v, preferred_element_type=F32)
    return o, m + jnp.log(den)


STRIDED_GROUPS = tuple(gi for gi, (_, d) in enumerate(DIL_GROUPS) if d > 1)
assert STRIDED_GROUPS == (1, 2) and all(w // d == Q_BLOCK for w, d in DIL_GROUPS)
STRIDED_UNROLL = 4


def _attn_strided_kernel(dil, q_ref, k_ref, v_ref, o_ref, l_ref):
    n_blk = SEQ // dil // Q_BLOCK
    n_keys = Q_BLOCK * min(n_blk, 2)
    dist0 = (lax.broadcasted_iota(jnp.int32, (Q_BLOCK, n_keys), 0)
             - lax.broadcasted_iota(jnp.int32, (Q_BLOCK, n_keys), 1))

    def tiles(it, carry):
        for u in range(STRIDED_UNROLL):
            tile = it * STRIDED_UNROLL + u
            r, n = tile // n_blk, tile % n_blk
            if n_blk == STRIDED_UNROLL:
                r, n = it, u
            first_key_blk = jnp.maximum(n - 1, 0)
            q_rows = pl.ds(r + dil * Q_BLOCK * n, Q_BLOCK, stride=dil)
            k_rows = pl.ds(r + dil * Q_BLOCK * first_key_blk, n_keys, stride=dil)
            dist = dist0 + Q_BLOCK * (n - first_key_blk)
            valid = jnp.where(dist >= 0, 1.0, 0.0) * jnp.where(dist <= Q_BLOCK, 1.0, 0.0) > 0.5
            o, lse = _softmax_pv(q_ref[q_rows, :].astype(BF16), k_ref[k_rows, :].astype(BF16),
                                 v_ref[k_rows, :].astype(BF16), valid)
            o_ref[q_rows, :] = o
            l_ref[q_rows, :] = jnp.broadcast_to(lse, (Q_BLOCK, HEAD_DIM))
        return carry

    lax.fori_loop(0, dil * n_blk // STRIDED_UNROLL, tiles, 0)


def _attn_strided(proj, gi):
    dil = DIL_GROUPS[gi][1]
    assert (SEQ // dil) % Q_BLOCK == 0 and (dil * (SEQ // dil // Q_BLOCK)) % STRIDED_UNROLL == 0
    heads_per_blk = PROJ_TN // HEAD_DIM
    src = lambda col: pl.BlockSpec(
        (SEQ, HEAD_DIM), lambda b, h: (b, (col + gi) * heads_per_blk + h))
    out = pl.BlockSpec((SEQ, HEAD_DIM), lambda b, h: (b, h))
    shape = jax.ShapeDtypeStruct((N_PROMPT, GROUP_WIDTH), F32)
    return pl.pallas_call(
        functools.partial(_attn_strided_kernel, dil),
        grid=(BATCH, HEADS_PER_GROUP),
        in_specs=[src(COL_Q), src(COL_K), src(COL_V)],
        out_specs=[out, out],
        out_shape=[shape, shape],
        compiler_params=_params(("parallel", "parallel")),
        name="attn_strided",
    )(proj, proj, proj)


def _attn_prompt_kernel(q_ref, k_ref, v_ref, o1_ref, l1_ref, o2_ref, l2_ref, o_ref):
    window = DIL_GROUPS[0][0]
    t0 = pl.multiple_of(pl.program_id(1) * Q_BLOCK, Q_BLOCK)
    n_keys = window + Q_BLOCK
    start = pl.multiple_of(jnp.maximum(t0 - window, 0), Q_BLOCK)
    dist = ((t0 - start) + lax.broadcasted_iota(jnp.int32, (Q_BLOCK, n_keys), 0)
            - lax.broadcasted_iota(jnp.int32, (Q_BLOCK, n_keys), 1))
    valid = jnp.where(dist >= 0, 1.0, 0.0) * jnp.where(dist <= window, 1.0, 0.0) > 0.5
    for h in range(HEADS_PER_GROUP):
        cs = slice(h * HEAD_DIM, (h + 1) * HEAD_DIM)
        o0, l0 = _softmax_pv(q_ref[:, cs].astype(BF16), k_ref[pl.ds(start, n_keys), cs].astype(BF16),
                             v_ref[pl.ds(start, n_keys), cs].astype(BF16), valid)
        l1, l2 = l1_ref[:, cs], l2_ref[:, cs]
        top = jnp.maximum(jnp.maximum(l0, l1), l2)
        w0, w1, w2 = jnp.exp(l0 - top), jnp.exp(l1 - top), jnp.exp(l2 - top)
        o_ref[:, cs] = (o0 * w0 + o1_ref[:, cs] * w1 + o2_ref[:, cs] * w2) / (w0 + w1 + w2)


def _attn_prompt(proj, strided):
    n_qb = SEQ // Q_BLOCK
    tile = pl.BlockSpec((Q_BLOCK, GROUP_WIDTH), lambda b, qb: (b * n_qb + qb, 0))
    (o1, l1), (o2, l2) = strided
    return pl.pallas_call(
        _attn_prompt_kernel,
        grid=(BATCH, n_qb),
        in_specs=[
            pl.BlockSpec((Q_BLOCK, GROUP_WIDTH), lambda b, qb: (b * n_qb + qb, COL_Q)),
            pl.BlockSpec((SEQ, GROUP_WIDTH), lambda b, qb: (b, COL_K)),
            pl.BlockSpec((SEQ, GROUP_WIDTH), lambda b, qb: (b, COL_V)),
            tile, tile, tile, tile,
        ],
        out_specs=tile,
        out_shape=jax.ShapeDtypeStruct((N_PROMPT, GROUP_WIDTH), F32),
        compiler_params=_params(("parallel", "arbitrary")),
        name="attn_prompt",
    )(proj, proj, proj, o1, l1, o2, l2)


SAMPLE_NB = 4


def _attn_sample_kernel(q_ref, kn_ref, vn_ref, k0_ref, v0_ref, k1_ref, v1_ref, k2_ref, v2_ref, o_ref):
    caches = ((k0_ref, v0_ref), (k1_ref, v1_ref), (k2_ref, v2_ref))

    def seq(n, carry):
        outs = []
        lses = []
        for g, (kc_ref, vc_ref) in enumerate(caches):
            q = q_ref[n, g]
            kc = kc_ref[n]
            s = jnp.sum(kc * q[None], axis=-1, keepdims=True) * ATTN_SCALE
            sn = jnp.sum(kn_ref[n, g] * q, axis=-1, keepdims=True) * ATTN_SCALE
            m = jnp.maximum(jnp.max(s, axis=0), sn)
            e = jnp.exp(s - m[None])
            en = jnp.exp(sn - m)
            den = jnp.sum(e, axis=0) + en
            o = (jnp.sum(e * vc_ref[n], axis=0) + en * vn_ref[n, g]) / den
            outs.append(o)
            lses.append(m + jnp.log(den))
        mx = jnp.maximum(jnp.maximum(lses[0], lses[1]), lses[2])
        ws = [jnp.exp(l - mx) for l in lses]
        tot = ws[0] + ws[1] + ws[2]
        o_ref[n] = (ws[0] * outs[0] + ws[1] * outs[1] + ws[2] * outs[2]) / tot
        return carry

    lax.fori_loop(0, SAMPLE_NB, seq, 0)


def _attn_sample(qs, kns, vns, caches):
    nb = SAMPLE_NB
    small = pl.BlockSpec((nb, len(DIL_GROUPS), HEADS_PER_GROUP, HEAD_DIM), lambda n: (n, 0, 0, 0))
    in_specs = [small, small, small]
    args = [qs, kns, vns]
    for c in caches:
        for kv in (0, 1):
            in_specs.append(pl.BlockSpec(
                (nb, N_KEYS, None, None, HEADS_PER_GROUP, HEAD_DIM),
                functools.partial(lambda n, kv_: (n, 0, 0, kv_, 0, 0), kv_=kv)))
            args.append(c)
    return pl.pallas_call(
        _attn_sample_kernel,
        grid=(DEC_BATCH // nb,),
        in_specs=in_specs,
        out_specs=pl.BlockSpec((nb, HEADS_PER_GROUP, HEAD_DIM), lambda n: (n, 0, 0)),
        out_shape=jax.ShapeDtypeStruct((DEC_BATCH, HEADS_PER_GROUP, HEAD_DIM), F32),
        compiler_params=_params(("parallel",)),
        name="attn_sample",
    )(*args)


TAIL_ROWS = N_PAD - N_PROMPT
MERGE_TM = TAIL_ROWS
N_PROMPT_BLOCKS = N_PROMPT // TAIL_ROWS


def _prompt_or_tail_specs(width):
    return [pl.BlockSpec((TAIL_ROWS, width), lambda i: (jnp.minimum(i, N_PROMPT_BLOCKS - 1), 0)),
            pl.BlockSpec((TAIL_ROWS, width), lambda i: (0, 0))]


def _prompt_or_tail(prompt_ref, tail_ref):
    return jnp.where(pl.program_id(0) < N_PROMPT_BLOCKS, prompt_ref[...], tail_ref[...])


def _merge_kernel(xp_ref, xt_ref, a_ref, bp_ref, bt_ref, ga_ref, gb_ref, wa_ref, wb_ref, wo_ref, o_ref):
    pa = jnp.dot(a_ref[...].astype(BF16), wa_ref[...], preferred_element_type=F32)
    pb = jnp.dot(_prompt_or_tail(bp_ref, bt_ref).astype(BF16), wb_ref[...], preferred_element_type=F32)
    merged = ga_ref[...] * pa + gb_ref[...] * pb
    o_ref[...] = _prompt_or_tail(xp_ref, xt_ref) + jnp.dot(merged.astype(BF16), wo_ref[...],
                                                          preferred_element_type=F32)


def _merge(x_prompt, x_tail, amix, bmix_prompt, bmix_tail, proj, wa, wb, wo):
    tm = MERGE_TM
    gate_a_blk = COL_GATE * PROJ_TN // D_MODEL
    return pl.pallas_call(
        _merge_kernel,
        grid=(N_PAD // tm,),
        in_specs=[
            *_prompt_or_tail_specs(D_MODEL),
            pl.BlockSpec((tm, A_WIDTH), lambda i: (i, 0)),
            *_prompt_or_tail_specs(GROUP_WIDTH),
            pl.BlockSpec((tm, D_MODEL), lambda i: (i, gate_a_blk)),
            pl.BlockSpec((tm, D_MODEL), lambda i: (i, gate_a_blk + 1)),
            _const_spec((A_WIDTH, D_MODEL)),
            _const_spec((GROUP_WIDTH, D_MODEL)),
            _const_spec((D_MODEL, D_MODEL)),
        ],
        out_specs=pl.BlockSpec((tm, D_MODEL), lambda i: (i, 0)),
        out_shape=jax.ShapeDtypeStruct((N_PAD, D_MODEL), F32),
        compiler_params=_params(("parallel",)),
        name="merge",
    )(x_prompt, x_tail, amix, bmix_prompt, bmix_tail, proj, proj, wa, wb, wo)


GATE_TB = 128
PAIRS_PER_RANK = tuple(PEER_TOPK // (a + 1) for a in range(PEER_TOPK))


SUBLANES = 8


def _batcher_pairs(n):
    pairs = []

    def merge(lo, m, r):
        step = 2 * r
        if step < m:
            merge(lo, m, step)
            merge(lo + r, m, step)
            pairs.extend((i, i + r) for i in range(lo + r, lo + m - r, step))
        else:
            pairs.append((lo, lo + r))

    def sort(lo, m):
        if m > 1:
            sort(lo, m // 2)
            sort(lo + m // 2, m // 2)
            merge(lo, m, 1)

    sort(0, n)
    return tuple(pairs)


SORT16_PAIRS = _batcher_pairs(PEER_TOPK)


def _exchange(v, i, j):
    v[i], v[j] = jnp.maximum(v[i], v[j]), jnp.minimum(v[i], v[j])


def _bitonic_sort_desc(c):
    c = list(c)
    d = PEER_TOPK // 2
    while d:
        for k in range(PEER_TOPK):
            if not k & d:
                _exchange(c, k, k + d)
        d //= 2
    return c


def _merge_sublanes(v, sort_last):
    for shift in (4, 2, 1):
        other = [pltpu.roll(x, shift, 0) for x in v]
        c = [jnp.maximum(v[k], other[PEER_TOPK - 1 - k]) for k in range(PEER_TOPK)]
        v = c if (shift == 1 and not sort_last) else _bitonic_sort_desc(c)
    return v


def _top16_sorted(s):
    v = [s[k * SUBLANES:(k + 1) * SUBLANES] for k in range(s.shape[0] // SUBLANES)]
    for i, j in SORT16_PAIRS:
        _exchange(v, i, j)
    return _merge_sublanes(v, True)


def _by_sublane(rows, sub8):
    out = rows[SUBLANES - 1]
    for b in range(SUBLANES - 2, -1, -1):
        out = jnp.where(sub8 == b, rows[b], out)
    return out


def _cast_once(pairs):
    @pl.when(pl.program_id(0) == 0)
    def _():
        for src_ref, dst_ref in pairs:
            dst_ref[...] = src_ref[...].astype(BF16)


def _peer_gate_kernel(x_ref, g_ref, wq32_ref, k132_ref, k232_ref,
                      xnt_ref, s1_ref, a_ref, s2_ref, bn_ref, tau_ref, wq_ref, k1_ref, k2_ref):
    _cast_once(((wq32_ref, wq_ref), (k132_ref, k1_ref), (k232_ref, k2_ref)))
    xn = _rms(x_ref[...], g_ref[...])
    xnt_ref[...] = xn.T.astype(BF16)
    qp = jnp.dot(xn.astype(BF16), wq_ref[...], preferred_element_type=F32).astype(BF16)
    sub8 = lax.broadcasted_iota(jnp.int32, (8, GATE_TB), 0)
    nt = (((1,), (1,)), ((), ()))
    for h in range(PEER_HEADS):
        c0 = h * 2 * PEER_HALF
        s1 = lax.dot_general(k1_ref[...], qp[:, c0:c0 + PEER_HALF], nt, preferred_element_type=F32)
        s2 = lax.dot_general(k2_ref[...], qp[:, c0 + PEER_HALF:c0 + 2 * PEER_HALF], nt,
                             preferred_element_type=F32)
        t1 = _top16_sorted(s1)
        t2 = _top16_sorted(s2)
        t2_lo = _by_sublane(t2[:SUBLANES], sub8)
        t2_hi = _by_sublane(t2[SUBLANES:], sub8)
        lo = []
        for a in range(PEER_TOPK):
            nvalid = min(SUBLANES, PAIRS_PER_RANK[a])
            c = t1[a] + t2_lo
            lo.append(c if nvalid == SUBLANES else jnp.where(sub8 < nvalid, c, -jnp.inf))
        hi = t1[0] + t2_hi
        ins = [jnp.maximum(lo[0], hi)]
        ins += [jnp.maximum(lo[k], jnp.minimum(lo[k - 1], hi)) for k in range(1, PEER_TOPK)]
        best = _merge_sublanes(ins, False)
        tau = functools.reduce(jnp.minimum, best)
        e2_lo = jnp.exp(t2_lo - t2[0])
        zacc = jnp.where(hi >= tau, jnp.exp(t2_hi - t2[0]), 0.0)
        for a in range(PEER_TOPK):
            zacc = zacc + jnp.where(lo[a] >= tau, jnp.exp(t1[a] - t1[0]) * e2_lo, 0.0)
        z = jnp.sum(zacc, axis=0, keepdims=True)
        s1_ref[h] = s1
        s2_ref[h] = s2
        a_ref[h] = jnp.exp(s1 - t1[0][0:1])
        bn_ref[h] = jnp.exp(s2 - t2[0][0:1]) / z
        tau_ref[h] = tau[0:1]


def _peer_gate(x1, g, wq, k1, k2):
    tb = GATE_TB
    big = pl.BlockSpec((PEER_HEADS, N_KEYS, tb), lambda i: (0, 0, i))
    big_shape = jax.ShapeDtypeStruct((PEER_HEADS, N_KEYS, N_PAD), F32)
    return pl.pallas_call(
        _peer_gate_kernel,
        grid=(N_PAD // tb,),
        in_specs=[
            pl.BlockSpec((tb, D_MODEL), lambda i: (i, 0)),
            pl.BlockSpec((1, D_MODEL), lambda i: (0, 0)),
            _const_spec((D_MODEL, PEER_HEADS * 2 * PEER_HALF)),
            _const_spec((N_KEYS, PEER_HALF)),
            _const_spec((N_KEYS, PEER_HALF)),
        ],
        out_specs=[
            pl.BlockSpec((D_MODEL, tb), lambda i: (0, i)),
            big, big, big, big,
            pl.BlockSpec((PEER_HEADS, 1, tb), lambda i: (0, 0, i)),
        ],
        out_shape=[
            jax.ShapeDtypeStruct((D_MODEL, N_PAD), BF16),
            big_shape, big_shape, big_shape, big_shape,
            jax.ShapeDtypeStruct((PEER_HEADS, 1, N_PAD), F32),
        ],
        scratch_shapes=[pltpu.VMEM((D_MODEL, PEER_HEADS * 2 * PEER_HALF), BF16),
                        pltpu.VMEM((N_KEYS, PEER_HALF), BF16), pltpu.VMEM((N_KEYS, PEER_HALF), BF16)],
        compiler_params=_params(("arbitrary",)),
        name="peer_gate",
    )(x1, g, wq, k1, k2)


DENSE_TB = 768
DENSE_EB = 1024
DENSE_EB_F32 = 512
N_DENSE_TOKEN_BLOCKS = N_PAD // DENSE_TB


def _peer_dense_kernel(from_f32, xnt_ref, s1_ref, a_ref, s2_ref, bn_ref, tau_ref, u_ref, v_ref, *rest):
    if from_f32:
        o_ref, ub_ref, vtb_ref, st_ref, wt_ref = rest
        u = u_ref[...].astype(BF16)
        v = v_ref[...].astype(BF16)
        ub_ref[...] = u
        vtb_ref[...] = v.T
    else:
        o_ref, st_ref, wt_ref = rest
    n_first = st_ref.shape[0] // N_KEYS

    @pl.when(pl.program_id(1) == 0)
    def _():
        o_ref[...] = jnp.zeros_like(o_ref)

    st_ref[...] = jnp.dot(u if from_f32 else u_ref[...], xnt_ref[...], preferred_element_type=F32)

    for ii in range(n_first):
        rows = slice(ii * N_KEYS, (ii + 1) * N_KEYS)
        for c in range(DENSE_TB // LANE):
            lanes = slice(c * LANE, (c + 1) * LANE)
            gate = jnp.zeros((N_KEYS, LANE), F32)
            for h in range(PEER_HEADS):
                s1_row = s1_ref[h, ii:ii + 1, lanes]
                a_row = a_ref[h, ii:ii + 1, lanes]
                hit = (s1_row + s2_ref[h, :, lanes]) >= tau_ref[h, :, lanes]
                gate = gate + jnp.where(hit, a_row * bn_ref[h, :, lanes], 0.0)
            act = jax.nn.gelu(st_ref[rows, lanes])
            wt_ref[rows, lanes] = (gate * act).astype(BF16)

    if from_f32:
        o_ref[...] += lax.dot_general(v, wt_ref[...], (((0,), (0,)), ((), ())), preferred_element_type=F32)
    else:
        o_ref[...] += jnp.dot(v_ref[...], wt_ref[...], preferred_element_type=F32)


def _peer_dense(xnt, s1t, at, s2t, bnt, tau, u_f32, v_f32):
    tb = DENSE_TB
    once = pl.Buffered(1)

    def token_specs(t_of):
        per_tok = lambda shape: pl.BlockSpec(shape, lambda t, e: (0,) * (len(shape) - 1) + (t_of(t),),
                                             pipeline_mode=once)
        return [per_tok((D_MODEL, tb)), per_tok((PEER_HEADS, N_KEYS, tb)), per_tok((PEER_HEADS, N_KEYS, tb)),
                per_tok((PEER_HEADS, 1, tb))]

    eb = DENSE_EB_F32
    ni = eb // N_KEYS
    n_e = N_EXPERTS // eb
    halves = DENSE_EB // eb
    first_keys = lambda a: a[:, :, :tb].reshape(PEER_HEADS, n_e, ni, tb)
    by_key0 = pl.BlockSpec((PEER_HEADS, None, ni, tb), lambda t, e: (0, e, 0, 0))
    xnt_s, s2_s, bn_s, tau_s = token_specs(lambda t: 0)
    out0, u_bf, vt_bf = pl.pallas_call(
        functools.partial(_peer_dense_kernel, True),
        grid=(1, n_e),
        in_specs=[xnt_s, by_key0, by_key0, s2_s, bn_s, tau_s,
                  pl.BlockSpec((eb, D_MODEL), lambda t, e: (e, 0)),
                  pl.BlockSpec((eb, D_MODEL), lambda t, e: (e, 0))],
        out_specs=[
            pl.BlockSpec((D_MODEL, tb), lambda t, e: (0, 0), pipeline_mode=once),
            pl.BlockSpec((eb, D_MODEL), lambda t, e: (e, 0)),
            pl.BlockSpec((None, D_MODEL, eb), lambda t, e: (e // halves, 0, e % halves)),
        ],
        out_shape=[
            jax.ShapeDtypeStruct((D_MODEL, tb), F32),
            jax.ShapeDtypeStruct((N_EXPERTS, D_MODEL), BF16),
            jax.ShapeDtypeStruct((N_EXPERTS // DENSE_EB, D_MODEL, DENSE_EB), BF16),
        ],
        scratch_shapes=[pltpu.VMEM((eb, tb), F32), pltpu.VMEM((eb, tb), BF16)],
        compiler_params=_params(("arbitrary", "arbitrary")),
        name="peer_dense_first",
    )(xnt, first_keys(s1t), first_keys(at), s2t, bnt, tau, u_f32, v_f32)

    eb = DENSE_EB
    by_key = pl.BlockSpec((PEER_HEADS, eb // N_KEYS, tb), lambda t, e: (0, e, t + 1))
    xnt_s, s2_s, bn_s, tau_s = token_specs(lambda t: t + 1)
    out_rest = pl.pallas_call(
        functools.partial(_peer_dense_kernel, False),
        grid=(N_DENSE_TOKEN_BLOCKS - 1, N_EXPERTS // eb),
        in_specs=[xnt_s, by_key, by_key, s2_s, bn_s, tau_s,
                  pl.BlockSpec((eb, D_MODEL), lambda t, e: (e, 0)),
                  pl.BlockSpec((None, D_MODEL, eb), lambda t, e: (e, 0, 0))],
        out_specs=pl.BlockSpec((D_MODEL, tb), lambda t, e: (0, t)),
        out_shape=jax.ShapeDtypeStruct((D_MODEL, N_PAD - tb), F32),
        scratch_shapes=[pltpu.VMEM((eb, tb), F32), pltpu.VMEM((eb, tb), BF16)],
        compiler_params=_params(("parallel", "arbitrary")),
        name="peer_dense",
    )(xnt, s1t, at, s2t, bnt, tau, u_bf, vt_bf)
    return out0, out_rest


FINAL_TM = 256
N_PROMPT_FINAL_BLOCKS = N_PROMPT // FINAL_TM


FINAL_FIRST_BLOCKS = DENSE_TB // FINAL_TM


def _final_kernel(x_ref, pt0_ref, pt_ref, pp_ref, ptail_ref, wg32_ref, wp32_ref, g_ref, yp_ref, ys_ref,
                  wg_ref, wp_ref):
    _cast_once(((wg32_ref, wg_ref), (wp32_ref, wp_ref)))
    i = pl.program_id(0)
    peer_t = jnp.where(i < FINAL_FIRST_BLOCKS, pt0_ref[...], pt_ref[...])
    x2 = x_ref[...] + peer_t.T
    gate = _sigmoid(jnp.dot(x2.astype(BF16), wg_ref[...], preferred_element_type=F32))
    emb = jnp.dot(_prompt_or_tail(pp_ref, ptail_ref).astype(BF16), wp_ref[...], preferred_element_type=F32)
    y = _rms(x2 + gate * emb, g_ref[...])

    @pl.when(i < N_PROMPT_FINAL_BLOCKS)
    def _():
        yp_ref[...] = y

    @pl.when(i == N_PROMPT_FINAL_BLOCKS)
    def _():
        ys_ref[...] = y[:DEC_BATCH]


def _final(x1, peer_t0, peer_t, p_prompt, p_tail, wg, wp, g):
    tm = FINAL_TM
    assert N_PAD // tm == N_PROMPT_FINAL_BLOCKS + 1 and DENSE_TB % tm == 0 and tm == TAIL_ROWS
    nf = FINAL_FIRST_BLOCKS
    return pl.pallas_call(
        _final_kernel,
        grid=(N_PAD // tm,),
        in_specs=[
            pl.BlockSpec((tm, D_MODEL), lambda i: (i, 0)),
            pl.BlockSpec((D_MODEL, tm), lambda i: (0, jnp.minimum(i, nf - 1))),
            pl.BlockSpec((D_MODEL, tm), lambda i: (0, jnp.maximum(i - nf, 0))),
            *_prompt_or_tail_specs(PLE_DIM),
            _const_spec((D_MODEL, D_MODEL)),
            _const_spec((PLE_DIM, D_MODEL)),
            pl.BlockSpec((1, D_MODEL), lambda i: (0, 0)),
        ],
        out_specs=[
            pl.BlockSpec((tm, D_MODEL), lambda i: (jnp.minimum(i, N_PROMPT_FINAL_BLOCKS - 1), 0)),
            pl.BlockSpec((DEC_BATCH, D_MODEL), lambda i: (0, 0)),
        ],
        out_shape=[
            jax.ShapeDtypeStruct((N_PROMPT, D_MODEL), F32),
            jax.ShapeDtypeStruct((DEC_BATCH, D_MODEL), F32),
        ],
        scratch_shapes=[pltpu.VMEM((D_MODEL, D_MODEL), BF16), pltpu.VMEM((PLE_DIM, D_MODEL), BF16)],
        compiler_params=_params(("arbitrary",)),
        name="final",
    )(x1, peer_t0, peer_t, p_prompt, p_tail, wg, wp, g)


KV_ROWS = 512


def _kv_store(k, v, o_ref):
    for h in range(HEADS_PER_GROUP):
        cs = slice(h * HEAD_DIM, (h + 1) * HEAD_DIM)
        o_ref[:, 0, h, :] = k[:, cs]
        o_ref[:, 1, h, :] = v[:, cs]


def _kv_prompt_kernel(k_ref, v_ref, o_ref):
    _kv_store(k_ref[...], v_ref[...], o_ref)


def _kv_prompt(proj, gi, rows):
    blk = min(rows, KV_ROWS)
    nblk = rows // blk
    first = SEQ // blk - nblk
    src = lambda col: pl.BlockSpec(
        (blk, GROUP_WIDTH), lambda b, r: (b * (SEQ // blk) + first + r, col + gi))
    return pl.pallas_call(
        _kv_prompt_kernel,
        grid=(BATCH, nblk),
        in_specs=[src(COL_K), src(COL_V)],
        out_specs=pl.BlockSpec((None, None, blk, 2, HEADS_PER_GROUP, HEAD_DIM),
                               lambda b, r: (0, b, r, 0, 0, 0)),
        out_shape=jax.ShapeDtypeStruct((1, BATCH, rows, 2, HEADS_PER_GROUP, HEAD_DIM), F32),
        compiler_params=_params(("parallel", "parallel")),
        name="kv_prompt",
    )(proj, proj)


def _kv_sample_kernel(k0_ref, v0_ref, k1_ref, v1_ref, k2_ref, v2_ref, o0_ref, o1_ref, o2_ref):
    _kv_store(k0_ref[...], v0_ref[...], o0_ref)
    _kv_store(k1_ref[...], v1_ref[...], o1_ref)
    _kv_store(k2_ref[...], v2_ref[...], o2_ref)


def _kv_sample(proj):
    row_blk = N_PROMPT // DEC_BATCH
    src = lambda col: pl.BlockSpec((DEC_BATCH, GROUP_WIDTH), lambda i: (row_blk, col))
    in_specs = []
    for gi in range(len(DIL_GROUPS)):
        in_specs += [src(COL_K + gi), src(COL_V + gi)]
    out_spec = pl.BlockSpec((None, DEC_BATCH, None, 2, HEADS_PER_GROUP, HEAD_DIM),
                            lambda i: (0, 0, 0, 0, 0, 0))
    out_shape = jax.ShapeDtypeStruct((1, DEC_BATCH, 1, 2, HEADS_PER_GROUP, HEAD_DIM), F32)
    return pl.pallas_call(
        _kv_sample_kernel,
        grid=(1,),
        in_specs=in_specs,
        out_specs=[out_spec] * 3,
        out_shape=[out_shape] * 3,
        compiler_params=_params(("arbitrary",)),
        name="kv_sample",
    )(*([proj] * 6))


def _rope_tables():
    pos = np.concatenate([np.tile(np.arange(SEQ), BATCH), np.full(DEC_BATCH, PAST_LEN), np.zeros(N_PAD - N_TOK)])
    inv = np.float32(ROPE_THETA) ** (-np.arange(ROT_HALF, dtype=np.float32) / np.float32(ROT_HALF))
    ang = pos.astype(np.float32)[:, None] * inv[None, :]
    cos, sin = np.cos(ang).astype(np.float32), np.sin(ang).astype(np.float32)
    n = pos.shape[0]
    zeros = lambda w: np.zeros((n, w), np.float32)
    c = np.concatenate([cos, cos, np.ones((n, HEAD_DIM - 2 * ROT_HALF), np.float32)], axis=1)
    s_hi = np.concatenate([-sin, zeros(HEAD_DIM - ROT_HALF)], axis=1)
    s_lo = np.concatenate([zeros(ROT_HALF), sin, zeros(HEAD_DIM - 2 * ROT_HALF)], axis=1)
    return jnp.asarray(c), jnp.asarray(s_hi), jnp.asarray(s_lo)


def kernel(x_prompt, x_sample, cache_kv_w128, cache_kv_w512, cache_kv_w2048, p_prompt, p_sample, g_mix, w_in, sgu_ln_g, sgu_ln_b, w_s, b_s, w_a_out, w_b_out, w_o, g_ffn, peer_w_q, peer_sub_k1, peer_sub_k2, peer_u, peer_v, w_ple, w_ple_gate, g_final):
    assert x_prompt.shape == (BATCH, SEQ, D_MODEL) and x_sample.shape == (DEC_BATCH, 1, D_MODEL)
    assert w_in.shape == (1, D_MODEL, IN_COLS)
    pad = N_PAD - N_TOK
    tail = lambda a: jnp.pad(a.reshape(DEC_BATCH, -1), ((0, pad), (0, 0)))
    x_p, x_t = x_prompt.reshape(N_PROMPT, D_MODEL), tail(x_sample)
    p_p, p_t = p_prompt.reshape(N_PROMPT, PLE_DIM), tail(p_sample)

    c, s_hi, s_lo = _rope_tables()
    proj = _proj(x_p, x_t, g_mix, w_in[0], c, s_hi, s_lo)

    bsb = jnp.broadcast_to(b_s[0][:, :, None], (A_GROUPS, CHUNK, CHUNK))
    wd = jnp.repeat(w_s[0, :, 0, 0], CHUNK)[None]
    b0 = jnp.repeat(b_s[0, :, 0], CHUNK)[None]
    amix, vn_s = _sgu(proj, sgu_ln_g, sgu_ln_b, w_s[0], bsb, wd, b0)

    bmix_p = _attn_prompt(proj, [_attn_strided(proj, gi) for gi in STRIDED_GROUPS])
    cq, ck, cv = COL_Q * PROJ_TN, COL_K * PROJ_TN, COL_V * PROJ_TN
    srows = proj[N_PROMPT:N_TOK]
    per_head = lambda a: a.reshape(DEC_BATCH, len(DIL_GROUPS), HEADS_PER_GROUP, HEAD_DIM)
    caches = []
    for cache, (window, dil) in zip((cache_kv_w128, cache_kv_w512, cache_kv_w2048), DIL_GROUPS):
        l_buf = cache.shape[2]
        assert l_buf == N_KEYS * dil
        caches.append(cache.reshape(DEC_BATCH, N_KEYS, dil, 2, HEADS_PER_GROUP, HEAD_DIM))
    bmix_s = _attn_sample(per_head(srows[:, cq:ck]), per_head(srows[:, ck:cv]), per_head(srows[:, cv:cv + B_WIDTH]), caches)
    x1 = _merge(x_p, x_t, amix, bmix_p, tail(bmix_s), proj, w_a_out[0].astype(BF16), w_b_out[0].astype(BF16), w_o[0].astype(BF16))

    xnt, s1t, at, s2t, bnt, tau = _peer_gate(x1, g_ffn, peer_w_q[0], peer_sub_k1[0], peer_sub_k2[0])
    peer_t0, peer_t = _peer_dense(xnt, s1t, at, s2t, bnt, tau, peer_u[0], peer_v[0])

    y_p, y_s = _final(x1, peer_t0, peer_t, p_p, p_t, w_ple_gate[0], w_ple[0], g_final[None])

    y_prompt = y_p.reshape(BATCH, SEQ, D_MODEL)
    y_sample = y_s.reshape(DEC_BATCH, 1, D_MODEL)
    kv_p = [_kv_prompt(proj, gi, min(window, SEQ)) for gi, (window, _) in enumerate(DIL_GROUPS)]
    kv_s = _kv_sample(proj)
    sgu_v_sample = vn_s.reshape(1, DEC_BATCH, 1, A_WIDTH)
    return (y_prompt, y_sample, kv_p[0], kv_p[1], kv_p[2], kv_s[0], kv_s[1], kv_s[2], sgu_v_sample)
```
